```python
import math
import jax
import jax.numpy as jnp
from jax import lax
import numpy as np

D_MODEL = 1024
BATCH = 4
SEQ = 4096
DEPTH = 2

SSD_HEADS = 16
SSD_HEAD_DIM = 64
SSD_INNER = SSD_HEADS * SSD_HEAD_DIM
SSD_GROUPS = 2
SSD_STATE = 128
SSD_CONV = 4
SSD_CHUNK = 128
SSD_CONV_CH = SSD_INNER + 2 * SSD_GROUPS * SSD_STATE
DT_MIN = 1e-3
DT_MAX = 1e-1

SGU_WIDTH = 1024
SGU_GROUPS = 8
SGU_GROUP_DIM = SGU_WIDTH // SGU_GROUPS
SGU_CHUNK = 128

MLA_HEADS = 16
MLA_Q_RANK = 256
MLA_KV_RANK = 128
MLA_NOPE = 64
MLA_ROPE = 32
MLA_V = 64
MLA_QK = MLA_NOPE + MLA_ROPE
MLA_WIDTH = MLA_HEADS * MLA_V
ATTN_BLOCK = 128
ROPE_BASE = 10000.0

N_BRANCH = 3
IN_SSD = 2 * SSD_INNER + 2 * SSD_GROUPS * SSD_STATE + SSD_HEADS
IN_SGU = 2 * SGU_WIDTH
IN_MLA = MLA_Q_RANK + MLA_KV_RANK + MLA_ROPE
IN_GATE = N_BRANCH * D_MODEL
N_IN = IN_SSD + IN_SGU + IN_MLA + IN_GATE

FFN_DENSE = 2816
N_EXPERTS = 8
TOP_K = 2
FFN_EXPERT = 3584
N_DENSE = (DEPTH + 1) // 2
N_MOE = DEPTH // 2

EPS = 1e-6
LN_EPS = 1e-5

kernel_name = 'hybrid_ssd_sgu_mla_moe_block'


def rms_norm(x, g, eps=EPS):
    xf = x.astype(jnp.float32)
    y = xf * lax.rsqrt(jnp.mean(xf * xf, axis=-1, keepdims=True) + eps)
    return (y * g.astype(jnp.float32)).astype(x.dtype)


def layer_norm(x, g, b, eps=LN_EPS):
    xf = x.astype(jnp.float32)
    mu = jnp.mean(xf, axis=-1, keepdims=True)
    xc = xf - mu
    y = xc * lax.rsqrt(jnp.mean(xc * xc, axis=-1, keepdims=True) + eps)
    return (y * g.astype(jnp.float32) + b.astype(jnp.float32)).astype(x.dtype)


def causal_depthwise_conv(x, w, b):
    ch = x.shape[-1]
    y = lax.conv_general_dilated(
        x, w[:, None, :].astype(x.dtype), window_strides=(1,),
        padding=[(SSD_CONV - 1, 0)], dimension_numbers=('NWC', 'WIO', 'NWC'),
        feature_group_count=ch)
    return y + b.astype(x.dtype)


def segsum_exp(a):
    l = a.shape[-1]
    cs = jnp.cumsum(a, axis=-1)
    diff = cs[..., :, None] - cs[..., None, :]
    mask = jnp.tril(jnp.ones((l, l), dtype=bool))
    return jnp.where(mask, jnp.exp(jnp.where(mask, diff, 0.0)), 0.0)


def ssd_scan(x, dt, a, b, c):
    bsz, s, h, p = x.shape
    g, n = b.shape[-2], b.shape[-1]
    k = h // g
    l = SSD_CHUNK
    nc = s // l
    xs = (x * dt[..., None]).reshape(bsz, nc, l, g, k, p)
    adt = jnp.moveaxis((dt * a).reshape(bsz, nc, l, g, k), 2, -1)
    a_cs = jnp.cumsum(adt, axis=-1)
    bc = b.reshape(bsz, nc, l, g, n)
    cc = c.reshape(bsz, nc, l, g, n)
    decay = segsum_exp(adt)
    cb = jnp.einsum('bclgn,bcsgn->bcgls', cc, bc)
    y_diag = jnp.einsum('bcgls,bcgkls,bcsgkp->bclgkp', cb, decay, xs)
    decay_states = jnp.exp(a_cs[..., -1:] - a_cs)
    states = jnp.einsum('bclgn,bcgkl,bclgkp->bcgkpn', bc, decay_states, xs)
    chunk_decay = jnp.exp(a_cs[..., -1])

    def step(carry, inp):
        st, dec = inp
        return carry * dec[..., None, None] + st, carry

    init = jnp.zeros((bsz, g, k, p, n), x.dtype)
    _, prev = lax.scan(step, init, (jnp.moveaxis(states, 1, 0), jnp.moveaxis(chunk_decay, 1, 0)))
    prev = jnp.moveaxis(prev, 0, 1)
    y_off = jnp.einsum('bclgn,bcgkpn,bcgkl->bclgkp', cc, prev, jnp.exp(a_cs))
    return (y_diag + y_off).reshape(bsz, s, h, p)


def ssd_branch(zxbcdt, conv_w, conv_b, dt_bias, a_log, d_skip, norm_g):
    bsz, s, _ = zxbcdt.shape
    f32 = jnp.float32
    z, xbc, dt = jnp.split(zxbcdt, [SSD_INNER, SSD_INNER + SSD_CONV_CH], axis=-1)
    xbc = jax.nn.silu(causal_depthwise_conv(xbc, conv_w, conv_b))
    xh, bm, cm = jnp.split(xbc, [SSD_INNER, SSD_INNER + SSD_GROUPS * SSD_STATE], axis=-1)
    xh = xh.reshape(bsz, s, SSD_HEADS, SSD_HEAD_DIM).astype(f32)
    bm = bm.reshape(bsz, s, SSD_GROUPS, SSD_STATE).astype(f32)
    cm = cm.reshape(bsz, s, SSD_GROUPS, SSD_STATE).astype(f32)
    dt = jax.nn.softplus(dt.astype(f32) + dt_bias.astype(f32))
    a = -jnp.exp(a_log.astype(f32))
    y = ssd_scan(xh, dt, a, bm, cm) + xh * d_skip.astype(f32)[:, None]
    yg = (y.reshape(bsz, s, SSD_INNER) * jax.nn.silu(z.astype(f32)))
    yg = yg.reshape(bsz, s, SSD_GROUPS, SSD_INNER // SSD_GROUPS)
    yg = yg * lax.rsqrt(jnp.mean(yg * yg, axis=-1, keepdims=True) + EPS)
    return (yg.reshape(bsz, s, SSD_INNER) * norm_g.astype(f32)).astype(zxbcdt.dtype)


def sgu_branch(uv, ln_g, ln_b, w_s, b_s):
    bsz, s, _ = uv.shape
    uv = jax.nn.gelu(uv, approximate=False)
    u, v = jnp.split(uv, 2, axis=-1)
    v = layer_norm(v, ln_g, ln_b)
    nc = s // SGU_CHUNK
    v = v.reshape(bsz, nc, SGU_CHUNK, SGU_GROUPS, SGU_GROUP_DIM)
    w = w_s * jnp.tril(jnp.ones((SGU_CHUNK, SGU_CHUNK), w_s.dtype))
    mixed = jnp.einsum('gts,bcsgd->bctgd', w, v) + jnp.swapaxes(b_s, 0, 1)[None, None, :, :, None]
    return u * mixed.reshape(bsz, s, SGU_WIDTH)


def rope_tables(s, dim):
    inv = 1.0 / (ROPE_BASE ** (jnp.arange(0, dim, 2, dtype=jnp.float32) / dim))
    ang = jnp.arange(s, dtype=jnp.float32)[:, None] * inv[None, :]
    return jnp.cos(ang), jnp.sin(ang)


def apply_rope(x, cos, sin):
    x1, x2 = jnp.split(x.astype(jnp.float32), 2, axis=-1)
    c = cos[None, :, None, :]
    sn = sin[None, :, None, :]
    return jnp.concatenate([x1 * c - x2 * sn, x1 * sn + x2 * c], axis=-1).astype(x.dtype)


def causal_attention(q, k, v):
    bsz, s, h, dq = q.shape
    nb = s // ATTN_BLOCK
    scale = dq ** -0.5
    qb = jnp.moveaxis(q.reshape(bsz, nb, ATTN_BLOCK, h, dq), 1, 0)
    k_pos = jnp.arange(s)

    def one_block(args):
        qi, start = args
        sc = jnp.einsum('bqhd,bkhd->bhqk', qi, k).astype(jnp.float32) * scale
        q_pos = start + jnp.arange(ATTN_BLOCK)
        mask = k_pos[None, :] <= q_pos[:, None]
        p = jax.nn.softmax(jnp.where(mask, sc, -jnp.inf), axis=-1)
        return jnp.einsum('bhqk,bkhd->bqhd', p.astype(v.dtype), v)

    out = lax.map(one_block, (qb, jnp.arange(nb) * ATTN_BLOCK))
    return jnp.moveaxis(out, 0, 1).reshape(bsz, s, h, v.shape[-1])


def mla_branch(lat, q_norm_g, w_uq, kv_norm_g, w_ukv, q_head_g, k_head_g):
    bsz, s, _ = lat.shape
    q_lat, kv_lat, k_pe = jnp.split(lat, [MLA_Q_RANK, MLA_Q_RANK + MLA_KV_RANK], axis=-1)
    q = (rms_norm(q_lat, q_norm_g) @ w_uq).reshape(bsz, s, MLA_HEADS, MLA_QK)
    kv = (rms_norm(kv_lat, kv_norm_g) @ w_ukv).reshape(bsz, s, MLA_HEADS, MLA_NOPE + MLA_V)
    k_nope, v = jnp.split(kv, [MLA_NOPE], axis=-1)
    k_pe_h = jnp.broadcast_to(k_pe[:, :, None, :], (bsz, s, MLA_HEADS, MLA_ROPE))
    k = jnp.concatenate([k_nope, k_pe_h], axis=-1)
    q = rms_norm(q, q_head_g)
    k = rms_norm(k, k_head_g)
    cos, sin = rope_tables(s, MLA_ROPE)
    q = jnp.concatenate([q[..., :MLA_NOPE], apply_rope(q[..., MLA_NOPE:], cos, sin)], axis=-1)
    k = jnp.concatenate([k[..., :MLA_NOPE], apply_rope(k[..., MLA_NOPE:], cos, sin)], axis=-1)
    out = causal_attention(q, k, v)
    return out.reshape(bsz, s, MLA_WIDTH)


def swiglu(x, wg, wu, wd):
    return (jax.nn.silu(x @ wg) * (x @ wu)) @ wd


def moe_ffn(x, router_w, wg, wu, wd):
    bsz, s, d = x.shape
    t = x.reshape(-1, d)
    logits = (t @ router_w).astype(jnp.float32)
    top_v, top_i = lax.top_k(logits, TOP_K)
    top_w = jax.nn.softmax(top_v, axis=-1)
    combine = jnp.sum(jax.nn.one_hot(top_i, N_EXPERTS, dtype=jnp.float32) * top_w[..., None], axis=1)
    y = jnp.zeros_like(t)
    for e in range(N_EXPERTS):
        y = y + combine[:, e:e + 1].astype(t.dtype) * swiglu(t, wg[e], wu[e], wd[e])
    return y.reshape(bsz, s, d)


def setup_inputs(seed: int = 0) -> dict:
    key = jax.random.key(seed)
    keys = iter(jax.random.split(key, 48))
    f32 = jnp.float32

    def normal(shape, scale):
        return jax.random.normal(next(keys), shape, f32) * scale

    def gain(shape):
        return 1.0 + 0.02 * jax.random.normal(next(keys), shape, f32)

    L = DEPTH
    dt = jnp.exp(jax.random.uniform(next(keys), (L, SSD_HEADS), f32, math.log(DT_MIN), math.log(DT_MAX)))
    inputs = {}
    inputs['x'] = normal((BATCH, SEQ, D_MODEL), 1.0)
    inputs['mix_norm_g'] = gain((L, D_MODEL))
    inputs['w_in'] = normal((L, D_MODEL, N_IN), D_MODEL ** -0.5)
    inputs['conv_w'] = normal((L, SSD_CONV, SSD_CONV_CH), SSD_CONV ** -0.5)
    inputs['conv_b'] = normal((L, SSD_CONV_CH), 0.02)
    inputs['dt_bias'] = dt + jnp.log(-jnp.expm1(-dt))
    inputs['a_log'] = jnp.log(jax.random.uniform(next(keys), (L, SSD_HEADS), f32, 1.0, 16.0))
    inputs['d_skip'] = gain((L, SSD_HEADS))
    inputs['ssd_norm_g'] = gain((L, SSD_INNER))
    inputs['sgu_ln_g'] = gain((L, SGU_WIDTH))
    inputs['sgu_ln_b'] = normal((L, SGU_WIDTH), 0.02)
    inputs['sgu_w'] = normal((L, SGU_GROUPS, SGU_CHUNK, SGU_CHUNK), SGU_CHUNK ** -0.5)
    inputs['sgu_b'] = gain((L, SGU_GROUPS, SGU_CHUNK))
    inputs['q_norm_g'] = gain((L, MLA_Q_RANK))
    inputs['w_uq'] = normal((L, MLA_Q_RANK, MLA_HEADS * MLA_QK), MLA_Q_RANK ** -0.5)
    inputs['kv_norm_g'] = gain((L, MLA_KV_RANK))
    inputs['w_ukv'] = normal((L, MLA_KV_RANK, MLA_HEADS * (MLA_NOPE + MLA_V)), MLA_KV_RANK ** -0.5)
    inputs['q_head_g'] = gain((L, MLA_QK))
    inputs['k_head_g'] = gain((L, MLA_QK))
    inputs['w_br_ssd'] = normal((L, SSD_INNER, D_MODEL), SSD_INNER ** -0.5)
    inputs['w_br_sgu'] = normal((L, SGU_WIDTH, D_MODEL), SGU_WIDTH ** -0.5)
    inputs['w_br_mla'] = normal((L, MLA_WIDTH, D_MODEL), MLA_WIDTH ** -0.5)
    inputs['w_out'] = normal((L, D_MODEL, D_MODEL), D_MODEL ** -0.5)
    inputs['ffn_norm_g'] = gain((L, D_MODEL))
    inputs['ffn_w_gate'] = normal((N_DENSE, D_MODEL, FFN_DENSE), D_MODEL ** -0.5)
    inputs['ffn_w_up'] = normal((N_DENSE, D_MODEL, FFN_DENSE), D_MODEL ** -0.5)
    inputs['ffn_w_down'] = normal((N_DENSE, FFN_DENSE, D_MODEL), FFN_DENSE ** -0.5)
    inputs['router_w'] = normal((N_MOE, D_MODEL, N_EXPERTS), D_MODEL ** -0.5)
    inputs['moe_w_gate'] = normal((N_MOE, N_EXPERTS, D_MODEL, FFN_EXPERT), D_MODEL ** -0.5)
    inputs['moe_w_up'] = normal((N_MOE, N_EXPERTS, D_MODEL, FFN_EXPERT), D_MODEL ** -0.5)
    inputs['moe_w_down'] = normal((N_MOE, N_EXPERTS, FFN_EXPERT, D_MODEL), FFN_EXPERT ** -0.5)
    return inputs


def reference(x, mix_norm_g, w_in, conv_w, conv_b, dt_bias, a_log, d_skip, ssd_norm_g,
              sgu_ln_g, sgu_ln_b, sgu_w, sgu_b, q_norm_g, w_uq, kv_norm_g, w_ukv,
              q_head_g, k_head_g, w_br_ssd, w_br_sgu, w_br_mla, w_out, ffn_norm_g,
              ffn_w_gate, ffn_w_up, ffn_w_down, router_w, moe_w_gate, moe_w_up, moe_w_down):
    for i in range(DEPTH):
        h = rms_norm(x, mix_norm_g[i])
        proj = h @ w_in[i]
        p_ssd, p_sgu, p_mla, p_gate = jnp.split(
            proj, [IN_SSD, IN_SSD + IN_SGU, IN_SSD + IN_SGU + IN_MLA], axis=-1)
        y_ssd = ssd_branch(p_ssd, conv_w[i], conv_b[i], dt_bias[i], a_log[i], d_skip[i], ssd_norm_g[i])
        y_sgu = sgu_branch(p_sgu, sgu_ln_g[i], sgu_ln_b[i], sgu_w[i], sgu_b[i])
        y_mla = mla_branch(p_mla, q_norm_g[i], w_uq[i], kv_norm_g[i], w_ukv[i], q_head_g[i], k_head_g[i])
        g_ssd, g_sgu, g_mla = jnp.split(jax.nn.sigmoid(p_gate), N_BRANCH, axis=-1)
        merged = (g_ssd * (y_ssd @ w_br_ssd[i])
                  + g_sgu * (y_sgu @ w_br_sgu[i])
                  + g_mla * (y_mla @ w_br_mla[i]))
        x = x + merged @ w_out[i]
        h = rms_norm(x, ffn_norm_g[i])
        j = i // 2
        if i % 2 == 0:
            x = x + swiglu(h, ffn_w_gate[j], ffn_w_up[j], ffn_w_down[j])
        else:
            x = x + moe_ffn(h, router_w[j], moe_w_gate[j], moe_w_up[j], moe_w_down[j])
    return x
```

```python
import functools
import math

import jax
import jax.numpy as jnp
from jax import lax
from jax.experimental import pallas as pl
from jax.experimental.pallas import tpu as pltpu

F32 = jnp.float32
BF16 = jnp.bfloat16

D_MODEL = 1024
SSD_HEADS = 16
SSD_HEAD_DIM = 64
SSD_INNER = SSD_HEADS * SSD_HEAD_DIM
SSD_GROUPS = 2
SSD_STATE = 128
SSD_CONV = 4
SSD_CHUNK = 128
SSD_CONV_CH = SSD_INNER + 2 * SSD_GROUPS * SSD_STATE
SGU_WIDTH = 1024
SGU_GROUPS = 8
SGU_CHUNK = 128
MLA_HEADS = 16
MLA_Q_RANK = 256
MLA_KV_RANK = 128
MLA_NOPE = 64
MLA_ROPE = 32
MLA_V = 64
MLA_QK = MLA_NOPE + MLA_ROPE
MLA_WIDTH = MLA_HEADS * MLA_V
ROPE_BASE = 10000.0
N_EXPERTS = 8
EPS = 1e-6
LN_EPS = 1e-5

LANE = 128
SUBLANE = 8
VMEM_LIMIT_BYTES = 48 * 1024 * 1024

SMALL_KPE_TILE = 3
SMALL_DT_TILE = 4
SMALL_WIDTH = 5 * LANE


def _params(*semantics):
    return pltpu.CompilerParams(dimension_semantics=semantics, vmem_limit_bytes=VMEM_LIMIT_BYTES)


def _sigmoid(x):
    return 1.0 / (1.0 + jnp.exp(-x))


def _rmsnorm_kernel(x_ref, g_ref, o_ref):
    x = x_ref[...]
    ms = jnp.mean(x * x, axis=-1, keepdims=True)
    o_ref[...] = (x * lax.rsqrt(ms + EPS) * g_ref[...]).astype(o_ref.dtype)


def rmsnorm(x, g, tm):
    t, d = x.shape
    return pl.pallas_call(
        _rmsnorm_kernel,
        grid=(t // tm,),
        in_specs=[pl.BlockSpec((tm, d), lambda i: (i, 0)), pl.BlockSpec((1, d), lambda i: (0, 0))],
        out_specs=pl.BlockSpec((tm, d), lambda i: (i, 0)),
        out_shape=jax.ShapeDtypeStruct((t, d), BF16),
        compiler_params=_params("parallel"),
        name="rmsnorm",
    )(x, g.reshape(1, d))


def _matmul_kernel(a_ref, w_ref, o_ref):
    o_ref[...] = jnp.dot(a_ref[...], w_ref[...], preferred_element_type=F32).astype(o_ref.dtype)


def matmul(a, w, out_dtype, tm, tn, name):
    t, k = a.shape
    n = w.shape[1]
    return pl.pallas_call(
        _matmul_kernel,
        grid=(t // tm, n // tn),
        in_specs=[pl.BlockSpec((tm, k), lambda i, j: (i, 0)), pl.BlockSpec((k, tn), lambda i, j: (0, j))],
        out_specs=pl.BlockSpec((tm, tn), lambda i, j: (i, j)),
        out_shape=jax.ShapeDtypeStruct((t, n), out_dtype),
        compiler_params=_params("parallel", "parallel"),
        name=name,
    )(a, w)


def _ssd_kernel(zx_ref, dt_ref, convw_ref, convb_ref, dtb_ref, alog_ref, dskip_ref, ng_ref, exp_ref,
                o_ref, xpad_ref, st_ref, *, chunks_per_step):
    l = SSD_CHUNK
    half = SSD_INNER // SSD_GROUPS

    @pl.when(pl.program_id(1) == 0)
    def _():
        xpad_ref[...] = jnp.zeros_like(xpad_ref)
        st_ref[...] = jnp.zeros_like(st_ref)

    row = lax.broadcasted_iota(jnp.int32, (l, l), 0)
    lane = lax.broadcasted_iota(jnp.int32, (l, l), 1)
    causal = row >= lane
    low_half = lane < SSD_HEAD_DIM
    tri = causal.astype(F32)
    a_neg = -jnp.exp(alog_ref[...])

    for ci in range(chunks_per_step):
        r0 = ci * l
        z = zx_ref[r0:r0 + l, 0:SSD_INNER].astype(F32)
        xpad_ref[0:SUBLANE, :] = xpad_ref[l:l + SUBLANE, :]
        xpad_ref[SUBLANE:SUBLANE + l, :] = zx_ref[r0:r0 + l, SSD_INNER:SSD_INNER + SSD_CONV_CH].astype(F32)
        conv = convb_ref[...]
        for j in range(SSD_CONV):
            s0 = SUBLANE - (SSD_CONV - 1) + j
            conv = conv + convw_ref[j:j + 1, :] * xpad_ref[s0:s0 + l, :]
        xbc = conv * _sigmoid(conv)
        xh = xbc[:, 0:SSD_INNER]
        bm = xbc[:, SSD_INNER:SSD_INNER + SSD_GROUPS * SSD_STATE].astype(BF16)
        cm = xbc[:, SSD_INNER + SSD_GROUPS * SSD_STATE:].astype(BF16)

        dt_in = dt_ref[r0:r0 + l, :] + dtb_ref[...]
        dt = jnp.maximum(dt_in, 0.0) + jnp.log1p(jnp.exp(-jnp.abs(dt_in)))
        adt = dt * a_neg
        cs = jnp.dot(tri, adt, preferred_element_type=F32, precision=lax.Precision.HIGHEST)
        cs_t = cs.T
        cs_last = cs[l - 1:l, :]
        small = jnp.concatenate([dt, jnp.exp(cs_last - cs), jnp.exp(cs)], axis=0).astype(BF16)
        wide = jnp.dot(small, exp_ref[...], preferred_element_type=F32)
        dt_e = wide[0:l]
        dstate_e = wide[l:2 * l]
        ecs_e = wide[2 * l:3 * l]
        cdec_e = ecs_e[l - 1:l, :]

        xs = xh * dt_e
        xs_b = xs.astype(BF16)
        xd_b = (xs * dstate_e).astype(BF16)

        y_parts = []
        off_parts = []
        for g in range(SSD_GROUPS):
            bg = bm[:, g * SSD_STATE:(g + 1) * SSD_STATE]
            cg = cm[:, g * SSD_STATE:(g + 1) * SSD_STATE]
            cb = lax.dot_general(cg, bg, (((1,), (1,)), ((), ())), preferred_element_type=F32)
            for hp in range(SSD_HEADS // SSD_GROUPS // 2):
                pair = g * (SSD_HEADS // SSD_GROUPS // 2) + hp
                xs_pair = xs_b[:, pair * LANE:(pair + 1) * LANE]
                ys = []
                for h in (2 * pair, 2 * pair + 1):
                    diff = cs[:, h:h + 1] - cs_t[h:h + 1, :]
                    decay = jnp.exp(jnp.where(causal, diff, -jnp.inf))
                    ys.append(jnp.dot((cb * decay).astype(BF16), xs_pair, preferred_element_type=F32))
                y_parts.append(jnp.where(low_half, ys[0], ys[1]))
            st_g = st_ref[:, g * half:(g + 1) * half]
            off_parts.append(jnp.dot(cg, st_g.astype(BF16), preferred_element_type=F32))
            new = lax.dot_general(bg, xd_b[:, g * half:(g + 1) * half], (((0,), (0,)), ((), ())),
                                  preferred_element_type=F32)
            st_ref[:, g * half:(g + 1) * half] = st_g * cdec_e[:, g * half:(g + 1) * half] + new

        y = (jnp.concatenate(y_parts, axis=1) + jnp.concatenate(off_parts, axis=1) * ecs_e
             + xh * dskip_ref[...])
        yg = y * (z * _sigmoid(z))
        outs = []
        for g in range(SSD_GROUPS):
            yh = yg[:, g * half:(g + 1) * half]
            ms = jnp.mean(yh * yh, axis=-1, keepdims=True)
            outs.append(yh * lax.rsqrt(ms + EPS))
        o_ref[r0:r0 + l, :] = (jnp.concatenate(outs, axis=1) * ng_ref[...]).astype(o_ref.dtype)


def ssd_branch(zx, small, conv_w, conv_b, dt_bias, a_log, d_skip, norm_g, batch, seq, chunks_per_step):
    rows = chunks_per_step * SSD_CHUNK
    steps = seq // rows
    pad_h = LANE - SSD_HEADS
    dtb = jnp.pad(dt_bias, (0, pad_h)).reshape(1, LANE)
    alog = jnp.pad(a_log, (0, pad_h)).reshape(1, LANE)
    dskip = jnp.repeat(d_skip, SSD_HEAD_DIM).reshape(1, SSD_INNER)
    expand = (jnp.arange(LANE)[:, None] == jnp.arange(SSD_INNER)[None, :] // SSD_HEAD_DIM).astype(BF16)
    const = lambda b, c: (0, 0)
    return pl.pallas_call(
        functools.partial(_ssd_kernel, chunks_per_step=chunks_per_step),
        grid=(batch, steps),
        in_specs=[
            pl.BlockSpec((rows, SSD_INNER + SSD_CONV_CH), lambda b, c: (b * steps + c, 0)),
            pl.BlockSpec((rows, LANE), lambda b, c: (b * steps + c, SMALL_DT_TILE)),
            pl.BlockSpec((SSD_CONV, SSD_CONV_CH), const),
            pl.BlockSpec((1, SSD_CONV_CH), const),
            pl.BlockSpec((1, LANE), const),
            pl.BlockSpec((1, LANE), const),
            pl.BlockSpec((1, SSD_INNER), const),
            pl.BlockSpec((1, SSD_INNER), const),
            pl.BlockSpec((LANE, SSD_INNER), const),
        ],
        out_specs=pl.BlockSpec((rows, SSD_INNER), lambda b, c: (b * steps + c, 0)),
        out_shape=jax.ShapeDtypeStruct((batch * seq, SSD_INNER), BF16),
        scratch_shapes=[
            pltpu.VMEM((SSD_CHUNK + SUBLANE, SSD_CONV_CH), F32),
            pltpu.VMEM((SSD_STATE, SSD_INNER), F32),
        ],
        compiler_params=_params("parallel", "arbitrary"),
        name="ssd_branch",
    )(zx, small, conv_w, conv_b.reshape(1, -1), dtb, alog, dskip, norm_g.reshape(1, -1), expand)


def _sgu_kernel(uv_ref, g_ref, b_ref, w_ref, bias_ref, o_ref, *, chunks_per_step):
    l = SGU_CHUNK
    uv = uv_ref[...].astype(F32)
    act = 0.5 * uv * (1.0 + lax.erf(uv * (1.0 / math.sqrt(2.0))))
    u = act[:, 0:SGU_WIDTH]
    v = act[:, SGU_WIDTH:]
    mu = jnp.mean(v, axis=-1, keepdims=True)
    vc = v - mu
    var = jnp.mean(vc * vc, axis=-1, keepdims=True)
    vn = (vc * lax.rsqrt(var + LN_EPS) * g_ref[...] + b_ref[...]).astype(BF16)
    row = lax.broadcasted_iota(jnp.int32, (l, l), 0)
    col = lax.broadcasted_iota(jnp.int32, (l, l), 1)
    causal = row >= col
    for g in range(SGU_GROUPS):
        wg = jnp.where(causal, w_ref[g], 0.0).astype(BF16)
        c0 = g * LANE
        for ci in range(chunks_per_step):
            r0 = ci * l
            mixed = jnp.dot(wg, vn[r0:r0 + l, c0:c0 + LANE], preferred_element_type=F32)
            o_ref[r0:r0 + l, c0:c0 + LANE] = (
                u[r0:r0 + l, c0:c0 + LANE] * (mixed + bias_ref[:, c0:c0 + LANE])).astype(o_ref.dtype)


def sgu_branch(uv, ln_g, ln_b, w_s, b_s, chunks_per_step):
    t = uv.shape[0]
    rows = chunks_per_step * SGU_CHUNK
    bias = jnp.repeat(b_s.T, SGU_WIDTH // SGU_GROUPS, axis=1)
    const2 = lambda i: (0, 0)
    return pl.pallas_call(
        functools.partial(_sgu_kernel, chunks_per_step=chunks_per_step),
        grid=(t // rows,),
        in_specs=[
            pl.BlockSpec((rows, 2 * SGU_WIDTH), lambda i: (i, 0)),
            pl.BlockSpec((1, SGU_WIDTH), const2),
            pl.BlockSpec((1, SGU_WIDTH), const2),
            pl.BlockSpec((SGU_GROUPS, SGU_CHUNK, SGU_CHUNK), lambda i: (0, 0, 0)),
            pl.BlockSpec((SGU_CHUNK, SGU_WIDTH), const2),
        ],
        out_specs=pl.BlockSpec((rows, SGU_WIDTH), lambda i: (i, 0)),
        out_shape=jax.ShapeDtypeStruct((t, SGU_WIDTH), BF16),
        compiler_params=_params("parallel"),
        name="sgu_branch",
    )(uv, ln_g.reshape(1, -1), ln_b.reshape(1, -1), w_s, bias)


def _mla_prep_kernel(small_ref, cos_ref, sin_ref, qng_ref, kvng_ref, qhg_ref, khg_ref, wq_ref, wk_ref, wv_ref,
                     q_ref, k_ref, v_ref):
    sm = small_ref[...]
    q_lat = sm[:, 0:MLA_Q_RANK]
    kv_lat = sm[:, MLA_Q_RANK:MLA_Q_RANK + MLA_KV_RANK]
    kpe = sm[:, SMALL_KPE_TILE * LANE:(SMALL_KPE_TILE + 1) * LANE]

    def rms(x, g):
        return x * lax.rsqrt(jnp.mean(x * x, axis=-1, keepdims=True) + EPS) * g

    ql = rms(q_lat, qng_ref[...]).astype(BF16)
    kvl = rms(kv_lat, kvng_ref[...]).astype(BF16)
    q_all = jnp.dot(ql, wq_ref[...], preferred_element_type=F32)
    k_all = jnp.dot(kvl, wk_ref[...], preferred_element_type=F32)
    v_ref[...] = jnp.dot(kvl, wv_ref[...], preferred_element_type=F32).astype(v_ref.dtype)

    cos = cos_ref[...]
    sin = sin_ref[...]
    lane = lax.broadcasted_iota(jnp.int32, cos.shape, 1)
    first_half = lane < MLA_NOPE + MLA_ROPE // 2
    inv_qk = 1.0 / MLA_QK

    def head_norm_rope(x, g):
        ms = jnp.sum(x * x, axis=-1, keepdims=True) * inv_qk
        xn = x * lax.rsqrt(ms + EPS) * g
        partner = jnp.where(first_half, pltpu.roll(xn, LANE - MLA_ROPE // 2, 1), pltpu.roll(xn, MLA_ROPE // 2, 1))
        return xn * cos + partner * sin

    scale = MLA_QK ** -0.5
    for h in range(MLA_HEADS):
        qh = head_norm_rope(q_all[:, h * LANE:(h + 1) * LANE], qhg_ref[...])
        q_ref[0, h] = (qh * scale).astype(q_ref.dtype)
        kh = head_norm_rope(k_all[:, h * LANE:(h + 1) * LANE] + kpe, khg_ref[...])
        k_ref[0, h] = kh.astype(k_ref.dtype)


def _pad_head_cols(w, width, pad_to):
    r = w.shape[0]
    w = w.reshape(r, MLA_HEADS, width)
    return jnp.pad(w, ((0, 0), (0, 0), (0, pad_to - width))).reshape(r, MLA_HEADS * pad_to)


def mla_prep(small, q_norm_g, w_uq, kv_norm_g, w_ukv, q_head_g, k_head_g, batch, seq, rows):
    steps = seq // rows
    wq = _pad_head_cols(w_uq, MLA_QK, LANE).astype(BF16)
    w_ukv3 = w_ukv.reshape(MLA_KV_RANK, MLA_HEADS, MLA_NOPE + MLA_V)
    wk = jnp.pad(w_ukv3[:, :, :MLA_NOPE], ((0, 0), (0, 0), (0, LANE - MLA_NOPE))).reshape(MLA_KV_RANK, -1).astype(BF16)
    wv = w_ukv3[:, :, MLA_NOPE:].reshape(MLA_KV_RANK, MLA_WIDTH).astype(BF16)
    qhg = jnp.pad(q_head_g, (0, LANE - MLA_QK)).reshape(1, LANE)
    khg = jnp.pad(k_head_g, (0, LANE - MLA_QK)).reshape(1, LANE)
    inv = 1.0 / (ROPE_BASE ** (jnp.arange(0, MLA_ROPE, 2, dtype=F32) / MLA_ROPE))
    ang = jnp.arange(seq, dtype=F32)[:, None] * inv[None, :]
    c, s = jnp.cos(ang), jnp.sin(ang)
    ones = jnp.ones((seq, MLA_NOPE), F32)
    zeros = jnp.zeros((seq, MLA_NOPE), F32)
    tail1 = jnp.ones((seq, LANE - MLA_QK), F32)
    tail0 = jnp.zeros((seq, LANE - MLA_QK), F32)
    cos_t = jnp.concatenate([ones, c, c, tail1], axis=1)
    sin_t = jnp.concatenate([zeros, -s, s, tail0], axis=1)
    const = lambda b, i: (0, 0)
    qk_shape = jax.ShapeDtypeStruct((batch, MLA_HEADS, seq, LANE), BF16)
    return pl.pallas_call(
        _mla_prep_kernel,
        grid=(batch, steps),
        in_specs=[
            pl.BlockSpec((rows, SMALL_WIDTH), lambda b, i: (b * steps + i, 0)),
            pl.BlockSpec((rows, LANE), lambda b, i: (i, 0)),
            pl.BlockSpec((rows, LANE), lambda b, i: (i, 0)),
            pl.BlockSpec((1, MLA_Q_RANK), const),
            pl.BlockSpec((1, MLA_KV_RANK), const),
            pl.BlockSpec((1, LANE), const),
            pl.BlockSpec((1, LANE), const),
            pl.BlockSpec((MLA_Q_RANK, MLA_HEADS * LANE), const),
            pl.BlockSpec((MLA_KV_RANK, MLA_HEADS * LANE), const),
            pl.BlockSpec((MLA_KV_RANK, MLA_WIDTH), const),
        ],
        out_specs=[
            pl.BlockSpec((1, MLA_HEADS, rows, LANE), lambda b, i: (b, 0, i, 0)),
            pl.BlockSpec((1, MLA_HEADS, rows, LANE), lambda b, i: (b, 0, i, 0)),
            pl.BlockSpec((rows, MLA_WIDTH), lambda b, i: (b * steps + i, 0)),
        ],
        out_shape=[qk_shape, qk_shape, jax.ShapeDtypeStruct((batch * seq, MLA_WIDTH), BF16)],
        compiler_params=_params("parallel", "parallel"),
        name="mla_prep",
    )(small, cos_t, sin_t, q_norm_g.reshape(1, -1), kv_norm_g.reshape(1, -1), qhg, khg, wq, wk, wv)


def _flash_kernel(qi_ref, kj_ref, q_ref, k_ref, v_ref, o_ref, m_ref, l_ref, acc_ref, *, blk):
    t = pl.program_id(2)
    qi = qi_ref[t]
    kj = kj_ref[t]

    @pl.when(kj == 0)
    def _():
        m_ref[...] = jnp.full_like(m_ref, -jnp.inf)
        l_ref[...] = jnp.zeros_like(l_ref)
        acc_ref[...] = jnp.zeros_like(acc_ref)

    def step(masked):
        v = v_ref[0]
        for hh in range(2):
            s = lax.dot_general(q_ref[0, hh], k_ref[0, hh], (((1,), (1,)), ((), ())),
                                preferred_element_type=F32)
            if masked:
                row = lax.broadcasted_iota(jnp.int32, (blk, blk), 0)
                col = lax.broadcasted_iota(jnp.int32, (blk, blk), 1)
                s = jnp.where(col <= row, s, -jnp.inf)
            m_prev = m_ref[hh]
            m_new = jnp.maximum(m_prev, jnp.max(s, axis=1, keepdims=True))
            p = jnp.exp(s - jnp.tile(m_new, (1, blk // LANE)))
            alpha = jnp.exp(m_prev - m_new)
            l_ref[hh] = alpha * l_ref[hh] + jnp.sum(p, axis=1, keepdims=True)
            acc_ref[hh] = alpha * acc_ref[hh] + jnp.dot(p.astype(BF16), v, preferred_element_type=F32)
            m_ref[hh] = m_new

    @pl.when(kj < qi)
    def _():
        step(False)

    @pl.when(kj == qi)
    def _():
        step(True)
        lane = lax.broadcasted_iota(jnp.int32, (blk, LANE), 1)
        out = jnp.where(lane < MLA_V, acc_ref[0] / l_ref[0], acc_ref[1] / l_ref[1])
        o_ref[0] = out.astype(o_ref.dtype)


def flash_attention(q, k, v, batch, seq, blk):
    nb = seq // blk
    pairs = [(i, j) for i in range(nb) for j in range(i + 1)]
    qi = jnp.asarray([p[0] for p in pairs], jnp.int32)
    kj = jnp.asarray([p[1] for p in pairs], jnp.int32)
    grid_spec = pltpu.PrefetchScalarGridSpec(
        num_scalar_prefetch=2,
        grid=(batch, MLA_HEADS // 2, len(pairs)),
        in_specs=[
            pl.BlockSpec((1, 2, blk, LANE), lambda b, h, t, qi, kj: (b, h, qi[t], 0)),
            pl.BlockSpec((1, 2, blk, LANE), lambda b, h, t, qi, kj: (b, h, kj[t], 0)),
            pl.BlockSpec((1, blk, LANE), lambda b, h, t, qi, kj: (b, kj[t], h)),
        ],
        out_specs=pl.BlockSpec((1, blk, LANE), lambda b, h, t, qi, kj: (b, qi[t], h)),
        scratch_shapes=[
            pltpu.VMEM((2, blk, LANE), F32),
            pltpu.VMEM((2, blk, LANE), F32),
            pltpu.VMEM((2, blk, LANE), F32),
        ],
    )
    return pl.pallas_call(
        functools.partial(_flash_kernel, blk=blk),
        grid_spec=grid_spec,
        out_shape=jax.ShapeDtypeStruct((batch, seq, MLA_WIDTH), BF16),
        compiler_params=_params("parallel", "parallel", "arbitrary"),
        name="mla_flash_attention",
    )(qi, kj, q, k, v.reshape(batch, seq, MLA_WIDTH))


def _merge_kernel(x_ref, ys_ref, yg_ref, ym_ref, gate_ref, ws_ref, wg_ref, wm_ref, wo_ref, g2_ref,
                  x1_ref, h2_ref):
    d = D_MODEL
    gate = _sigmoid(gate_ref[...].astype(F32))
    merged = (gate[:, 0:d] * jnp.dot(ys_ref[...], ws_ref[...], preferred_element_type=F32)
              + gate[:, d:2 * d] * jnp.dot(yg_ref[...], wg_ref[...], preferred_element_type=F32)
              + gate[:, 2 * d:] * jnp.dot(ym_ref[...], wm_ref[...], preferred_element_type=F32))
    x1 = x_ref[...] + jnp.dot(merged.astype(BF16), wo_ref[...], preferred_element_type=F32)
    x1_ref[...] = x1
    ms = jnp.mean(x1 * x1, axis=-1, keepdims=True)
    h2_ref[...] = (x1 * lax.rsqrt(ms + EPS) * g2_ref[...]).astype(h2_ref.dtype)


def merge(x, y_ssd, y_sgu, y_mla, gate, w_ssd, w_sgu, w_mla, w_out, ffn_norm_g, tm):
    t, d = x.shape
    row = lambda i: (i, 0)
    const = lambda i: (0, 0)
    act = pl.BlockSpec((tm, d), row)
    wspec = pl.BlockSpec((d, d), const)
    return pl.pallas_call(
        _merge_kernel,
        grid=(t // tm,),
        in_specs=[act, act, act, act, pl.BlockSpec((tm, 3 * d), row), wspec, wspec, wspec, wspec,
                  pl.BlockSpec((1, d), const)],
        out_specs=[act, act],
        out_shape=[jax.ShapeDtypeStruct((t, d), F32), jax.ShapeDtypeStruct((t, d), BF16)],
        compiler_params=_params("parallel"),
        name="merge_out_proj",
    )(x, y_ssd, y_sgu, y_mla, gate, w_ssd.astype(BF16), w_sgu.astype(BF16), w_mla.astype(BF16),
      w_out.astype(BF16), ffn_norm_g.reshape(1, d))


def _ffn_kernel(h_ref, x_ref, c_ref, wg_ref, wu_ref, wd_ref, o_ref, *, use_combine):
    e = pl.program_id(1)
    f = pl.program_id(2)

    @pl.when((e == 0) & (f == 0))
    def _():
        o_ref[...] = x_ref[...]

    h = h_ref[...]
    a = jnp.dot(h, wg_ref[0], preferred_element_type=F32)
    b = jnp.dot(h, wu_ref[0], preferred_element_type=F32)
    act = ((a * _sigmoid(a)) * b).astype(BF16)
    y = jnp.dot(act, wd_ref[0], preferred_element_type=F32)
    if use_combine:
        lane = lax.broadcasted_iota(jnp.int32, c_ref.shape, 1)
        y = y * jnp.sum(jnp.where(lane == e, c_ref[...], 0.0), axis=-1, keepdims=True)
    o_ref[...] += y


def ffn(h, x, combine, wg, wu, wd, tm, tf):
    t, d = x.shape
    n_e, _, f_dim = wg.shape
    use_combine = combine is not None
    if combine is None:
        combine = jnp.ones((t, LANE), F32)
    row = lambda i, e, f: (i, 0)
    return pl.pallas_call(
        functools.partial(_ffn_kernel, use_combine=use_combine),
        grid=(t // tm, n_e, f_dim // tf),
        in_specs=[
            pl.BlockSpec((tm, d), row),
            pl.BlockSpec((tm, d), row),
            pl.BlockSpec((tm, LANE), row),
            pl.BlockSpec((1, d, tf), lambda i, e, f: (e, 0, f)),
            pl.BlockSpec((1, d, tf), lambda i, e, f: (e, 0, f)),
            pl.BlockSpec((1, tf, d), lambda i, e, f: (e, f, 0)),
        ],
        out_specs=pl.BlockSpec((tm, d), row),
        out_shape=jax.ShapeDtypeStruct((t, d), F32),
        compiler_params=_params("parallel", "arbitrary", "arbitrary"),
        name="swiglu_ffn",
    )(h, x, combine, wg.astype(BF16), wu.astype(BF16), wd.astype(BF16))


def _router_kernel(x_ref, g_ref, rw_ref, c_ref):
    x = x_ref[...]
    h = x * lax.rsqrt(jnp.mean(x * x, axis=-1, keepdims=True) + EPS) * g_ref[...]
    logits = jnp.dot(h, rw_ref[...], preferred_element_type=F32, precision=lax.Precision.HIGHEST)
    lane = lax.broadcasted_iota(jnp.int32, logits.shape, 1)
    lg = jnp.where(lane < N_EXPERTS, logits, -jnp.inf)
    m1 = jnp.max(lg, axis=-1, keepdims=True)
    i1 = jnp.min(jnp.where(lg == m1, lane, LANE), axis=-1, keepdims=True)
    lg2 = jnp.where(lane == i1, -jnp.inf, lg)
    m2 = jnp.max(lg2, axis=-1, keepdims=True)
    i2 = jnp.min(jnp.where(lg2 == m2, lane, LANE), axis=-1, keepdims=True)
    e2 = jnp.exp(m2 - m1)
    w1 = 1.0 / (1.0 + e2)
    w2 = e2 / (1.0 + e2)
    c_ref[...] = jnp.where(lane == i1, w1, 0.0) + jnp.where(lane == i2, w2, 0.0)


def router(x, g, router_w, tm):
    t, d = x.shape
    rw = jnp.pad(router_w, ((0, 0), (0, LANE - N_EXPERTS)))
    return pl.pallas_call(
        _router_kernel,
        grid=(t // tm,),
        in_specs=[pl.BlockSpec((tm, d), lambda i: (i, 0)), pl.BlockSpec((1, d), lambda i: (0, 0)),
                  pl.BlockSpec((d, LANE), lambda i: (0, 0))],
        out_specs=pl.BlockSpec((tm, LANE), lambda i: (i, 0)),
        out_shape=jax.ShapeDtypeStruct((t, LANE), F32),
        compiler_params=_params("parallel"),
        name="moe_router",
    )(x, g.reshape(1, d), rw)


def _pack_in_proj(w):
    o = 0
    z = w[:, o:o + SSD_INNER]; o += SSD_INNER
    xbc = w[:, o:o + SSD_CONV_CH]; o += SSD_CONV_CH
    dt = w[:, o:o + SSD_HEADS]; o += SSD_HEADS
    uv = w[:, o:o + 2 * SGU_WIDTH]; o += 2 * SGU_WIDTH
    q_lat = w[:, o:o + MLA_Q_RANK]; o += MLA_Q_RANK
    kv_lat = w[:, o:o + MLA_KV_RANK]; o += MLA_KV_RANK
    k_pe = w[:, o:o + MLA_ROPE]; o += MLA_ROPE
    gate = w[:, o:]
    d = w.shape[0]
    zeros = lambda n: jnp.zeros((d, n), w.dtype)
    small = jnp.concatenate([
        q_lat, kv_lat,
        zeros(MLA_NOPE), k_pe, zeros(LANE - MLA_QK),
        dt, zeros(LANE - SSD_HEADS)], axis=1)
    return (jnp.concatenate([z, xbc], axis=1).astype(BF16), uv.astype(BF16), gate.astype(BF16),
            small.astype(BF16))


def _pick(n, prefs):
    for p in prefs:
        if n % p == 0:
            return p
    return n


def kernel(x, mix_norm_g, w_in, conv_w, conv_b, dt_bias, a_log, d_skip, ssd_norm_g, sgu_ln_g, sgu_ln_b, sgu_w,
           sgu_b, q_norm_g, w_uq, kv_norm_g, w_ukv, q_head_g, k_head_g, w_br_ssd, w_br_sgu, w_br_mla, w_out,
           ffn_norm_g, ffn_w_gate, ffn_w_up, ffn_w_down, router_w, moe_w_gate, moe_w_up, moe_w_down):
    batch, seq, d = x.shape
    t = batch * seq
    depth = w_in.shape[0]
    tm = _pick(t, (1024, 512, 256, 128))
    xf = x.reshape(t, d)
    for i in range(depth):
        w_zx, w_uv, w_gate, w_small = _pack_in_proj(w_in[i])
        hn = rmsnorm(xf, mix_norm_g[i], tm)
        zx = matmul(hn, w_zx, BF16, tm, _pick(w_zx.shape[1], (1280,)), "in_proj_ssd")
        uv = matmul(hn, w_uv, BF16, tm, _pick(w_uv.shape[1], (1024,)), "in_proj_sgu")
        gate = matmul(hn, w_gate, BF16, tm, _pick(w_gate.shape[1], (1536,)), "in_proj_gate")
        small = matmul(hn, w_small, F32, tm, SMALL_WIDTH, "in_proj_small")
        y_ssd = ssd_branch(zx, small, conv_w[i], conv_b[i], dt_bias[i], a_log[i], d_skip[i], ssd_norm_g[i],
                           batch, seq, _pick(seq // SSD_CHUNK, (2, 1)))
        y_sgu = sgu_branch(uv, sgu_ln_g[i], sgu_ln_b[i], sgu_w[i], sgu_b[i], _pick(seq // SGU_CHUNK, (2, 1)))
        q, k, v = mla_prep(small, q_norm_g[i], w_uq[i], kv_norm_g[i], w_ukv[i], q_head_g[i], k_head_g[i],
                           batch, seq, _pick(seq, (256, 128)))
        y_mla = flash_attention(q, k, v, batch, seq, _pick(seq, (512, 256, 128))).reshape(t, MLA_WIDTH)
        x1, h2 = merge(xf, y_ssd, y_sgu, y_mla, gate, w_br_ssd[i], w_br_sgu[i], w_br_mla[i], w_out[i],
                       ffn_norm_g[i], _pick(t, (512, 256, 128)))
        j = i // 2
        if i % 2 == 0:
            xf = ffn(h2, x1, None, ffn_w_gate[j][None], ffn_w_up[j][None], ffn_w_down[j][None], tm,
                     _pick(ffn_w_gate.shape[2], (256,)))
        else:
            combine = router(x1, ffn_norm_g[i], router_w[j], tm)
            xf = ffn(h2, x1, combine, moe_w_gate[j], moe_w_up[j], moe_w_down[j], tm,
                     _pick(moe_w_gate.shape[3], (512, 256)))
    return xf.reshape(batch, seq, d)
```

```python
import functools
import math

import jax
import jax.numpy as jnp
from jax import lax
from jax.experimental import pallas as pl
from jax.experimental.pallas import tpu as pltpu

F32 = jnp.float32
BF16 = jnp.bfloat16

D_MODEL = 1024
SSD_HEADS = 16
SSD_HEAD_DIM = 64
SSD_INNER = SSD_HEADS * SSD_HEAD_DIM
SSD_GROUPS = 2
SSD_STATE = 128
SSD_CONV = 4
SSD_CHUNK = 128
SSD_CONV_CH = SSD_INNER + 2 * SSD_GROUPS * SSD_STATE
SGU_WIDTH = 1024
SGU_GROUPS = 8
SGU_CHUNK = 128
MLA_HEADS = 16
MLA_Q_RANK = 256
MLA_KV_RANK = 128
MLA_NOPE = 64
MLA_ROPE = 32
MLA_V = 64
MLA_QK = MLA_NOPE + MLA_ROPE
MLA_WIDTH = MLA_HEADS * MLA_V
ROPE_BASE = 10000.0
N_EXPERTS = 8
EPS = 1e-6
LN_EPS = 1e-5

LANE = 128
SUBLANE = 8
VMEM_LIMIT_BYTES = 48 * 1024 * 1024
MOE_VMEM_LIMIT_BYTES = 56 * 1024 * 1024

ROUTE_E1, ROUTE_E2, ROUTE_W1, ROUTE_W2 = 0, 1, 2, 3

SMALL_KPE_TILE = 3
SMALL_KPEP_TILE = 4
SMALL_DT_TILE = 5
SMALL_WIDTH = 6 * LANE


def _params(*semantics):
    return pltpu.CompilerParams(dimension_semantics=semantics, vmem_limit_bytes=VMEM_LIMIT_BYTES)


def _sigmoid(x):
    return 1.0 / (1.0 + jnp.exp(-x))


def _rmsnorm_kernel(x_ref, g_ref, o_ref):
    x = x_ref[...]
    ms = jnp.mean(x * x, axis=-1, keepdims=True)
    o_ref[...] = (x * lax.rsqrt(ms + EPS) * g_ref[...]).astype(o_ref.dtype)


def rmsnorm(x, g, tm):
    t, d = x.shape
    return pl.pallas_call(
        _rmsnorm_kernel,
        grid=(t // tm,),
        in_specs=[pl.BlockSpec((tm, d), lambda i: (i, 0)), pl.BlockSpec((1, d), lambda i: (0, 0))],
        out_specs=pl.BlockSpec((tm, d), lambda i: (i, 0)),
        out_shape=jax.ShapeDtypeStruct((t, d), BF16),
        compiler_params=_params("parallel"),
        name="rmsnorm",
    )(x, g.reshape(1, d))


def _matmul_kernel(a_ref, w_ref, o_ref):
    o_ref[...] = jnp.dot(a_ref[...], w_ref[...], preferred_element_type=F32).astype(o_ref.dtype)


def matmul(a, w, out_dtype, tm, tn, name):
    t, k = a.shape
    n = w.shape[1]
    return pl.pallas_call(
        _matmul_kernel,
        grid=(t // tm, n // tn),
        in_specs=[pl.BlockSpec((tm, k), lambda i, j: (i, 0)), pl.BlockSpec((k, tn), lambda i, j: (0, j))],
        out_specs=pl.BlockSpec((tm, tn), lambda i, j: (i, j)),
        out_shape=jax.ShapeDtypeStruct((t, n), out_dtype),
        compiler_params=_params("parallel", "parallel"),
        name=name,
    )(a, w)


def _ssd_kernel(zx_ref, dt_ref, convw_ref, convb_ref, dtb_ref, alog_ref, dskip_ref, ng_ref, exp_ref,
                o_ref, xpad_ref, st_ref, *, chunks_per_step):
    l = SSD_CHUNK
    half = SSD_INNER // SSD_GROUPS

    @pl.when(pl.program_id(1) == 0)
    def _():
        xpad_ref[...] = jnp.zeros_like(xpad_ref)
        st_ref[...] = jnp.zeros_like(st_ref)

    row = lax.broadcasted_iota(jnp.int32, (l, l), 0)
    lane = lax.broadcasted_iota(jnp.int32, (l, l), 1)
    causal = row >= lane
    low_half = lane < SSD_HEAD_DIM
    tri = causal.astype(F32)
    a_neg = -jnp.exp(alog_ref[...])

    for ci in range(chunks_per_step):
        r0 = ci * l
        z = zx_ref[r0:r0 + l, 0:SSD_INNER].astype(F32)
        xpad_ref[0:SUBLANE, :] = xpad_ref[l:l + SUBLANE, :]
        xpad_ref[SUBLANE:SUBLANE + l, :] = zx_ref[r0:r0 + l, SSD_INNER:SSD_INNER + SSD_CONV_CH].astype(F32)
        conv = convb_ref[...]
        for j in range(SSD_CONV):
            s0 = SUBLANE - (SSD_CONV - 1) + j
            conv = conv + convw_ref[j:j + 1, :] * xpad_ref[s0:s0 + l, :]
        xbc = conv * _sigmoid(conv)
        xh = xbc[:, 0:SSD_INNER]
        bm = xbc[:, SSD_INNER:SSD_INNER + SSD_GROUPS * SSD_STATE].astype(BF16)
        cm = xbc[:, SSD_INNER + SSD_GROUPS * SSD_STATE:].astype(BF16)

        dt_in = dt_ref[r0:r0 + l, :] + dtb_ref[...]
        dt = jnp.maximum(dt_in, 0.0) + jnp.log1p(jnp.exp(-jnp.abs(dt_in)))
        adt = dt * a_neg
        cs = jnp.dot(tri, adt, preferred_element_type=F32, precision=lax.Precision.HIGHEST)
        cs_t = cs.T
        cs_last = cs[l - 1:l, :]
        small = jnp.concatenate([dt, jnp.exp(cs_last - cs), jnp.exp(cs)], axis=0).astype(BF16)
        wide = jnp.dot(small, exp_ref[...], preferred_element_type=F32)
        dt_e = wide[0:l]
        dstate_e = wide[l:2 * l]
        ecs_e = wide[2 * l:3 * l]
        cdec_e = ecs_e[l - 1:l, :]

        xs = xh * dt_e
        xs_b = xs.astype(BF16)
        xd_b = (xs * dstate_e).astype(BF16)

        y_parts = []
        off_parts = []
        for g in range(SSD_GROUPS):
            bg = bm[:, g * SSD_STATE:(g + 1) * SSD_STATE]
            cg = cm[:, g * SSD_STATE:(g + 1) * SSD_STATE]
            cb = lax.dot_general(cg, bg, (((1,), (1,)), ((), ())), preferred_element_type=F32)
            for hp in range(SSD_HEADS // SSD_GROUPS // 2):
                pair = g * (SSD_HEADS // SSD_GROUPS // 2) + hp
                xs_pair = xs_b[:, pair * LANE:(pair + 1) * LANE]
                ys = []
                for h in (2 * pair, 2 * pair + 1):
                    diff = cs[:, h:h + 1] - cs_t[h:h + 1, :]
                    decay = jnp.exp(jnp.where(causal, diff, -jnp.inf))
                    ys.append(jnp.dot((cb * decay).astype(BF16), xs_pair, preferred_element_type=F32))
                y_parts.append(jnp.where(low_half, ys[0], ys[1]))
            st_g = st_ref[:, g * half:(g + 1) * half]
            off_parts.append(jnp.dot(cg, st_g.astype(BF16), preferred_element_type=F32))
            new = lax.dot_general(bg, xd_b[:, g * half:(g + 1) * half], (((0,), (0,)), ((), ())),
                                  preferred_element_type=F32)
            st_ref[:, g * half:(g + 1) * half] = st_g * cdec_e[:, g * half:(g + 1) * half] + new

        y = (jnp.concatenate(y_parts, axis=1) + jnp.concatenate(off_parts, axis=1) * ecs_e
             + xh * dskip_ref[...])
        yg = y * (z * _sigmoid(z))
        outs = []
        for g in range(SSD_GROUPS):
            yh = yg[:, g * half:(g + 1) * half]
            ms = jnp.mean(yh * yh, axis=-1, keepdims=True)
            outs.append(yh * lax.rsqrt(ms + EPS))
        o_ref[r0:r0 + l, :] = (jnp.concatenate(outs, axis=1) * ng_ref[...]).astype(o_ref.dtype)


def ssd_branch(zx, small, conv_w, conv_b, dt_bias, a_log, d_skip, norm_g, batch, seq, chunks_per_step):
    rows = chunks_per_step * SSD_CHUNK
    steps = seq // rows
    pad_h = LANE - SSD_HEADS
    dtb = jnp.pad(dt_bias, (0, pad_h)).reshape(1, LANE)
    alog = jnp.pad(a_log, (0, pad_h)).reshape(1, LANE)
    dskip = jnp.repeat(d_skip, SSD_HEAD_DIM).reshape(1, SSD_INNER)
    expand = (jnp.arange(LANE)[:, None] == jnp.arange(SSD_INNER)[None, :] // SSD_HEAD_DIM).astype(BF16)
    const = lambda b, c: (0, 0)
    return pl.pallas_call(
        functools.partial(_ssd_kernel, chunks_per_step=chunks_per_step),
        grid=(batch, steps),
        in_specs=[
            pl.BlockSpec((rows, SSD_INNER + SSD_CONV_CH), lambda b, c: (b * steps + c, 0)),
            pl.BlockSpec((rows, LANE), lambda b, c: (b * steps + c, SMALL_DT_TILE)),
            pl.BlockSpec((SSD_CONV, SSD_CONV_CH), const),
            pl.BlockSpec((1, SSD_CONV_CH), const),
            pl.BlockSpec((1, LANE), const),
            pl.BlockSpec((1, LANE), const),
            pl.BlockSpec((1, SSD_INNER), const),
            pl.BlockSpec((1, SSD_INNER), const),
            pl.BlockSpec((LANE, SSD_INNER), const),
        ],
        out_specs=pl.BlockSpec((rows, SSD_INNER), lambda b, c: (b * steps + c, 0)),
        out_shape=jax.ShapeDtypeStruct((batch * seq, SSD_INNER), BF16),
        scratch_shapes=[
            pltpu.VMEM((SSD_CHUNK + SUBLANE, SSD_CONV_CH), F32),
            pltpu.VMEM((SSD_STATE, SSD_INNER), F32),
        ],
        compiler_params=_params("parallel", "arbitrary"),
        name="ssd_branch",
    )(zx, small, conv_w, conv_b.reshape(1, -1), dtb, alog, dskip, norm_g.reshape(1, -1), expand)


def _sgu_kernel(uv_ref, g_ref, b_ref, w_ref, bias_ref, o_ref, *, chunks_per_step):
    l = SGU_CHUNK
    uv = uv_ref[...].astype(F32)
    act = 0.5 * uv * (1.0 + lax.erf(uv * (1.0 / math.sqrt(2.0))))
    u = act[:, 0:SGU_WIDTH]
    v = act[:, SGU_WIDTH:]
    mu = jnp.mean(v, axis=-1, keepdims=True)
    vc = v - mu
    var = jnp.mean(vc * vc, axis=-1, keepdims=True)
    vn = (vc * lax.rsqrt(var + LN_EPS) * g_ref[...] + b_ref[...]).astype(BF16)
    row = lax.broadcasted_iota(jnp.int32, (l, l), 0)
    col = lax.broadcasted_iota(jnp.int32, (l, l), 1)
    causal = row >= col
    for g in range(SGU_GROUPS):
        wg = jnp.where(causal, w_ref[g], 0.0).astype(BF16)
        c0 = g * LANE
        for ci in range(chunks_per_step):
            r0 = ci * l
            mixed = jnp.dot(wg, vn[r0:r0 + l, c0:c0 + LANE], preferred_element_type=F32)
            o_ref[r0:r0 + l, c0:c0 + LANE] = (
                u[r0:r0 + l, c0:c0 + LANE] * (mixed + bias_ref[:, c0:c0 + LANE])).astype(o_ref.dtype)


def sgu_branch(uv, ln_g, ln_b, w_s, b_s, chunks_per_step):
    t = uv.shape[0]
    rows = chunks_per_step * SGU_CHUNK
    bias = jnp.repeat(b_s.T, SGU_WIDTH // SGU_GROUPS, axis=1)
    const2 = lambda i: (0, 0)
    return pl.pallas_call(
        functools.partial(_sgu_kernel, chunks_per_step=chunks_per_step),
        grid=(t // rows,),
        in_specs=[
            pl.BlockSpec((rows, 2 * SGU_WIDTH), lambda i: (i, 0)),
            pl.BlockSpec((1, SGU_WIDTH), const2),
            pl.BlockSpec((1, SGU_WIDTH), const2),
            pl.BlockSpec((SGU_GROUPS, SGU_CHUNK, SGU_CHUNK), lambda i: (0, 0, 0)),
            pl.BlockSpec((SGU_CHUNK, SGU_WIDTH), const2),
        ],
        out_specs=pl.BlockSpec((rows, SGU_WIDTH), lambda i: (i, 0)),
        out_shape=jax.ShapeDtypeStruct((t, SGU_WIDTH), BF16),
        compiler_params=_params("parallel"),
        name="sgu_branch",
    )(uv, ln_g.reshape(1, -1), ln_b.reshape(1, -1), w_s, bias)


def _mla_prep_kernel(small_ref, cos_ref, sin_ref, qng_ref, kvng_ref, qhg_ref, khg_ref, wq_ref, wqp_ref, wk_ref,
                     wv_ref, q_ref, k_ref, v_ref):
    sm = small_ref[...]
    q_lat = sm[:, 0:MLA_Q_RANK]
    kv_lat = sm[:, MLA_Q_RANK:MLA_Q_RANK + MLA_KV_RANK]
    kpe = sm[:, SMALL_KPE_TILE * LANE:(SMALL_KPE_TILE + 1) * LANE]
    kpe_p = sm[:, SMALL_KPEP_TILE * LANE:(SMALL_KPEP_TILE + 1) * LANE]

    def rms(x, g):
        return x * lax.rsqrt(jnp.mean(x * x, axis=-1, keepdims=True) + EPS) * g

    ql = rms(q_lat, qng_ref[...]).astype(BF16)
    kvl = rms(kv_lat, kvng_ref[...]).astype(BF16)
    v_ref[...] = jnp.dot(kvl, wv_ref[...], preferred_element_type=F32).astype(v_ref.dtype)

    ones_blk = jnp.ones((LANE, LANE), BF16)

    def head_inv_rms(x):
        sq = (x * x).astype(BF16)
        ssq = jnp.concatenate(
            [jnp.dot(sq[:, h * LANE:(h + 1) * LANE], ones_blk, preferred_element_type=F32)
             for h in range(MLA_HEADS)], axis=1)
        return lax.rsqrt(ssq * (1.0 / MLA_QK) + EPS)

    cos = cos_ref[...]
    sin = sin_ref[...]
    rep = lambda a: jnp.tile(a, (1, MLA_HEADS))

    q_scale = MLA_QK ** -0.5 * math.log2(math.e)
    q_x = jnp.dot(ql, wq_ref[...], preferred_element_type=F32)
    q_xp = jnp.dot(ql, wqp_ref[...], preferred_element_type=F32)
    q_a = cos * qhg_ref[0:1, :] * q_scale
    q_b = sin * qhg_ref[1:2, :] * q_scale
    q_out = head_inv_rms(q_x) * (q_x * rep(q_a) + q_xp * rep(q_b))
    k_x = jnp.dot(kvl, wk_ref[...], preferred_element_type=F32) + rep(kpe)
    k_a = cos * khg_ref[0:1, :]
    k_b = sin * khg_ref[1:2, :]
    k_out = head_inv_rms(k_x) * (k_x * rep(k_a) + rep(kpe_p * k_b))
    for h in range(MLA_HEADS):
        q_ref[0, h] = q_out[:, h * LANE:(h + 1) * LANE].astype(q_ref.dtype)
        k_ref[0, h] = k_out[:, h * LANE:(h + 1) * LANE].astype(k_ref.dtype)


def _swap_rope_halves(a):
    half = MLA_ROPE // 2
    return jnp.concatenate([a[..., half:], a[..., :half]], axis=-1)


def _head_gain_rows(g):
    direct = jnp.pad(g, (0, LANE - MLA_QK))
    partner = jnp.pad(_swap_rope_halves(g[MLA_NOPE:]), (MLA_NOPE, LANE - MLA_QK))
    return jnp.stack([direct, partner])


def mla_prep(small, q_norm_g, w_uq, kv_norm_g, w_ukv, q_head_g, k_head_g, batch, seq, rows):
    steps = seq // rows
    w_uq3 = w_uq.reshape(MLA_Q_RANK, MLA_HEADS, MLA_QK)
    wq = jnp.pad(w_uq3, ((0, 0), (0, 0), (0, LANE - MLA_QK))).reshape(MLA_Q_RANK, -1).astype(BF16)
    wqp = jnp.pad(_swap_rope_halves(w_uq3[:, :, MLA_NOPE:]), ((0, 0), (0, 0), (MLA_NOPE, LANE - MLA_QK)))
    wqp = wqp.reshape(MLA_Q_RANK, -1).astype(BF16)
    w_ukv3 = w_ukv.reshape(MLA_KV_RANK, MLA_HEADS, MLA_NOPE + MLA_V)
    wk = jnp.pad(w_ukv3[:, :, :MLA_NOPE], ((0, 0), (0, 0), (0, LANE - MLA_NOPE))).reshape(MLA_KV_RANK, -1).astype(BF16)
    wv = w_ukv3[:, :, MLA_NOPE:].reshape(MLA_KV_RANK, MLA_WIDTH).astype(BF16)
    qhg = _head_gain_rows(q_head_g)
    khg = _head_gain_rows(k_head_g)
    inv = 1.0 / (ROPE_BASE ** (jnp.arange(0, MLA_ROPE, 2, dtype=F32) / MLA_ROPE))
    ang = jnp.arange(seq, dtype=F32)[:, None] * inv[None, :]
    c, s = jnp.cos(ang), jnp.sin(ang)
    ones = jnp.ones((seq, MLA_NOPE), F32)
    zeros = jnp.zeros((seq, MLA_NOPE), F32)
    tail1 = jnp.ones((seq, LANE - MLA_QK), F32)
    tail0 = jnp.zeros((seq, LANE - MLA_QK), F32)
    cos_t = jnp.concatenate([ones, c, c, tail1], axis=1)
    sin_t = jnp.concatenate([zeros, -s, s, tail0], axis=1)
    const = lambda b, i: (0, 0)
    qk_shape = jax.ShapeDtypeStruct((batch, MLA_HEADS, seq, LANE), BF16)
    return pl.pallas_call(
        _mla_prep_kernel,
        grid=(batch, steps),
        in_specs=[
            pl.BlockSpec((rows, SMALL_WIDTH), lambda b, i: (b * steps + i, 0)),
            pl.BlockSpec((rows, LANE), lambda b, i: (i, 0)),
            pl.BlockSpec((rows, LANE), lambda b, i: (i, 0)),
            pl.BlockSpec((1, MLA_Q_RANK), const),
            pl.BlockSpec((1, MLA_KV_RANK), const),
            pl.BlockSpec((2, LANE), const),
            pl.BlockSpec((2, LANE), const),
            pl.BlockSpec((MLA_Q_RANK, MLA_HEADS * LANE), const),
            pl.BlockSpec((MLA_Q_RANK, MLA_HEADS * LANE), const),
            pl.BlockSpec((MLA_KV_RANK, MLA_HEADS * LANE), const),
            pl.BlockSpec((MLA_KV_RANK, MLA_WIDTH), const),
        ],
        out_specs=[
            pl.BlockSpec((1, MLA_HEADS, rows, LANE), lambda b, i: (b, 0, i, 0)),
            pl.BlockSpec((1, MLA_HEADS, rows, LANE), lambda b, i: (b, 0, i, 0)),
            pl.BlockSpec((rows, MLA_WIDTH), lambda b, i: (b * steps + i, 0)),
        ],
        out_shape=[qk_shape, qk_shape, jax.ShapeDtypeStruct((batch * seq, MLA_WIDTH), BF16)],
        compiler_params=_params("parallel", "parallel"),
        name="mla_prep",
    )(small, cos_t, sin_t, q_norm_g.reshape(1, -1), kv_norm_g.reshape(1, -1), qhg, khg, wq, wqp, wk, wv)


FLASH_HEADS_PER_STEP = 4


def _flash_kernel(qi_ref, kj_ref, q_ref, k_ref, v_ref, o_ref, m_ref, acc_ref, *, blk):
    t = pl.program_id(2)
    qi = qi_ref[t]
    kj = kj_ref[t]
    heads = FLASH_HEADS_PER_STEP

    @pl.when(kj == 0)
    def _():
        m_ref[...] = jnp.full_like(m_ref, -jnp.inf)
        acc_ref[...] = jnp.zeros_like(acc_ref)

    def step(masked):
        first = lax.broadcasted_iota(jnp.int32, (blk, LANE), 1) < MLA_V
        one = jnp.ones((blk, LANE), BF16)
        for hh in range(heads):
            v = v_ref[0, :, (hh // 2) * LANE:(hh // 2 + 1) * LANE]
            v_ones = jnp.where(first, v, one) if hh % 2 == 0 else jnp.where(first, one, v)
            s = lax.dot_general(q_ref[0, hh], k_ref[0, hh], (((1,), (1,)), ((), ())),
                                preferred_element_type=F32)
            if masked:
                row = lax.broadcasted_iota(jnp.int32, (blk, blk), 0)
                col = lax.broadcasted_iota(jnp.int32, (blk, blk), 1)
                s = jnp.where(col <= row, s, -jnp.inf)
            m_prev = m_ref[hh]
            m_new = jnp.maximum(m_prev, jnp.max(s, axis=1, keepdims=True))
            p = jnp.exp2(s - jnp.tile(m_new, (1, blk // LANE)))
            alpha = jnp.exp2(m_prev - m_new)
            acc_ref[hh] = alpha * acc_ref[hh] + jnp.dot(p.astype(BF16), v_ones, preferred_element_type=F32)
            m_ref[hh] = m_new

    @pl.when(kj < qi)
    def _():
        step(False)

    @pl.when(kj == qi)
    def _():
        step(True)
        lane = lax.broadcasted_iota(jnp.int32, (blk, LANE), 1)
        for pair in range(heads // 2):
            a0 = acc_ref[2 * pair]
            a1 = acc_ref[2 * pair + 1]
            out = jnp.where(lane < MLA_V, a0 / pltpu.roll(a0, MLA_V, 1), a1 / pltpu.roll(a1, MLA_V, 1))
            o_ref[0, :, pair * LANE:(pair + 1) * LANE] = out.astype(o_ref.dtype)


def flash_attention(q, k, v, batch, seq, blk):
    nb = seq // blk
    heads = FLASH_HEADS_PER_STEP
    width = heads * MLA_V
    pairs = [(i, j) for i in range(nb) for j in range(i + 1)]
    qi = jnp.asarray([p[0] for p in pairs], jnp.int32)
    kj = jnp.asarray([p[1] for p in pairs], jnp.int32)
    grid_spec = pltpu.PrefetchScalarGridSpec(
        num_scalar_prefetch=2,
        grid=(batch, MLA_HEADS // heads, len(pairs)),
        in_specs=[
            pl.BlockSpec((1, heads, blk, LANE), lambda b, h, t, qi, kj: (b, h, qi[t], 0)),
            pl.BlockSpec((1, heads, blk, LANE), lambda b, h, t, qi, kj: (b, h, kj[t], 0)),
            pl.BlockSpec((1, blk, width), lambda b, h, t, qi, kj: (b, kj[t], h)),
        ],
        out_specs=pl.BlockSpec((1, blk, width), lambda b, h, t, qi, kj: (b, qi[t], h)),
        scratch_shapes=[
            pltpu.VMEM((heads, blk, LANE), F32),
            pltpu.VMEM((heads, blk, LANE), F32),
        ],
    )
    return pl.pallas_call(
        functools.partial(_flash_kernel, blk=blk),
        grid_spec=grid_spec,
        out_shape=jax.ShapeDtypeStruct((batch, seq, MLA_WIDTH), BF16),
        compiler_params=_params("parallel", "parallel", "arbitrary"),
        name="mla_flash_attention",
    )(qi, kj, q, k, v.reshape(batch, seq, MLA_WIDTH))


def _merge_kernel(x_ref, ys_ref, yg_ref, ym_ref, gate_ref, ws_ref, wg_ref, wm_ref, wo_ref, g2_ref,
                  x1_ref, h2_ref):
    d = D_MODEL
    gate = _sigmoid(gate_ref[...].astype(F32))
    merged = (gate[:, 0:d] * jnp.dot(ys_ref[...], ws_ref[...], preferred_element_type=F32)
              + gate[:, d:2 * d] * jnp.dot(yg_ref[...], wg_ref[...], preferred_element_type=F32)
              + gate[:, 2 * d:] * jnp.dot(ym_ref[...], wm_ref[...], preferred_element_type=F32))
    x1 = x_ref[...] + jnp.dot(merged.astype(BF16), wo_ref[...], preferred_element_type=F32)
    x1_ref[...] = x1
    ms = jnp.mean(x1 * x1, axis=-1, keepdims=True)
    h2_ref[...] = (x1 * lax.rsqrt(ms + EPS) * g2_ref[...]).astype(h2_ref.dtype)


def merge(x, y_ssd, y_sgu, y_mla, gate, w_ssd, w_sgu, w_mla, w_out, ffn_norm_g, tm, h2_dtype):
    t, d = x.shape
    row = lambda i: (i, 0)
    const = lambda i: (0, 0)
    act = pl.BlockSpec((tm, d), row)
    wspec = pl.BlockSpec((d, d), const)
    return pl.pallas_call(
        _merge_kernel,
        grid=(t // tm,),
        in_specs=[act, act, act, act, pl.BlockSpec((tm, 3 * d), row), wspec, wspec, wspec, wspec,
                  pl.BlockSpec((1, d), const)],
        out_specs=[act, act],
        out_shape=[jax.ShapeDtypeStruct((t, d), F32), jax.ShapeDtypeStruct((t, d), h2_dtype)],
        compiler_params=_params("parallel"),
        name="merge_out_proj",
    )(x, y_ssd, y_sgu, y_mla, gate, w_ssd.astype(BF16), w_sgu.astype(BF16), w_mla.astype(BF16),
      w_out.astype(BF16), ffn_norm_g.reshape(1, d))


def _ffn_kernel(h_ref, x_ref, wg_ref, wu_ref, wd_ref, o_ref):
    @pl.when(pl.program_id(1) == 0)
    def _():
        o_ref[...] = x_ref[...]

    h = h_ref[...]
    a = jnp.dot(h, wg_ref[...], preferred_element_type=F32)
    b = jnp.dot(h, wu_ref[...], preferred_element_type=F32)
    act = ((a * _sigmoid(a)) * b).astype(BF16)
    o_ref[...] += jnp.dot(act, wd_ref[...], preferred_element_type=F32)


def ffn(h, x, wg, wu, wd, tm, tf):
    t, d = x.shape
    f_dim = wg.shape[1]
    row = lambda i, f: (i, 0)
    return pl.pallas_call(
        _ffn_kernel,
        grid=(t // tm, f_dim // tf),
        in_specs=[
            pl.BlockSpec((tm, d), row),
            pl.BlockSpec((tm, d), row),
            pl.BlockSpec((d, tf), lambda i, f: (0, f)),
            pl.BlockSpec((d, tf), lambda i, f: (0, f)),
            pl.BlockSpec((tf, d), lambda i, f: (f, 0)),
        ],
        out_specs=pl.BlockSpec((tm, d), row),
        out_shape=jax.ShapeDtypeStruct((t, d), F32),
        compiler_params=_params("parallel", "arbitrary"),
        name="swiglu_ffn",
    )(h, x, wg.astype(BF16), wu.astype(BF16), wd.astype(BF16))


def _router_kernel(x_ref, g_ref, rw_ref, c_ref):
    x = x_ref[...]
    h = x * lax.rsqrt(jnp.mean(x * x, axis=-1, keepdims=True) + EPS) * g_ref[...]
    logits = jnp.dot(h, rw_ref[...], preferred_element_type=F32, precision=lax.Precision.HIGHEST)
    lane = lax.broadcasted_iota(jnp.int32, logits.shape, 1)
    lg = jnp.where(lane < N_EXPERTS, logits, -jnp.inf)
    m1 = jnp.max(lg, axis=-1, keepdims=True)
    i1 = jnp.min(jnp.where(lg == m1, lane, LANE), axis=-1, keepdims=True)
    lg2 = jnp.where(lane == i1, -jnp.inf, lg)
    m2 = jnp.max(lg2, axis=-1, keepdims=True)
    i2 = jnp.min(jnp.where(lg2 == m2, lane, LANE), axis=-1, keepdims=True)
    e2 = jnp.exp(m2 - m1)
    w1 = 1.0 / (1.0 + e2)
    w2 = e2 / (1.0 + e2)
    c_ref[...] = (jnp.where(lane == ROUTE_E1, i1.astype(F32), 0.0) + jnp.where(lane == ROUTE_E2, i2.astype(F32), 0.0)
                  + jnp.where(lane == ROUTE_W1, w1, 0.0) + jnp.where(lane == ROUTE_W2, w2, 0.0))


def router(x, g, router_w, tm):
    t, d = x.shape
    rw = jnp.pad(router_w, ((0, 0), (0, LANE - N_EXPERTS)))
    return pl.pallas_call(
        _router_kernel,
        grid=(t // tm,),
        in_specs=[pl.BlockSpec((tm, d), lambda i: (i, 0)), pl.BlockSpec((1, d), lambda i: (0, 0)),
                  pl.BlockSpec((d, LANE), lambda i: (0, 0))],
        out_specs=pl.BlockSpec((tm, LANE), lambda i: (i, 0)),
        out_shape=jax.ShapeDtypeStruct((t, LANE), F32),
        compiler_params=_params("parallel"),
        name="moe_router",
    )(x, g.reshape(1, d), rw)


def _moe_plan(route, t, tm):
    n_assign = 2 * t
    n_tiles_max = n_assign // tm + N_EXPERTS
    e = jnp.concatenate([route[:, ROUTE_E1], route[:, ROUTE_E2]]).astype(jnp.int32)
    onehot = (e[:, None] == jnp.arange(N_EXPERTS, dtype=jnp.int32)[None, :]).astype(jnp.int32)
    csum = jnp.cumsum(onehot, axis=0)
    rank = jnp.sum(csum * onehot, axis=1) - 1
    counts = csum[-1]
    tiles_per = (counts + tm - 1) // tm
    tile_end = jnp.cumsum(tiles_per)
    tile_start = tile_end - tiles_per
    n_tiles = tile_end[-1]
    pos = tile_start[e] * tm + rank
    dest = jnp.zeros((n_tiles_max * tm,), jnp.int32).at[pos].set(jnp.arange(n_assign, dtype=jnp.int32))
    tok = dest % t
    tile = jnp.arange(n_tiles_max, dtype=jnp.int32)
    tile_expert = jnp.sum((tile[:, None] >= tile_end[None, :]).astype(jnp.int32), axis=1)
    last_expert = jnp.sum((n_tiles - 1 >= tile_end).astype(jnp.int32))
    tile_expert = jnp.where(tile < n_tiles, tile_expert, last_expert)
    tile_rows = jnp.clip(counts[tile_expert] - (tile - tile_start[tile_expert]) * tm, 0, tm)
    tile_rows = jnp.where(tile < n_tiles, tile_rows, 0).astype(jnp.int32)
    shape3 = (n_tiles_max, 1, tm)
    return tile_expert, tile_rows, tok.reshape(shape3), dest.reshape(shape3)


def _moe_group_kernel(texp_ref, rows_ref, tok_ref, tok_next_ref, dest_ref, h_hbm, wg_ref, wu_ref, wd_ref,
                      y_hbm, xbuf, ybuf, gsem, ssem, *, tm, tf):
    del texp_ref
    i = pl.program_id(0)
    last = pl.num_programs(0) - 1
    cur = i % 2
    f_dim = wg_ref.shape[2]
    rows = rows_ref[i]
    rows_prev = rows_ref[jnp.maximum(i - 1, 0)]
    rows_next = rows_ref[jnp.minimum(i + 1, last)]

    def gather_copy(token, r, buf):
        return pltpu.make_async_copy(h_hbm.at[pl.ds(token, 1), :], xbuf.at[buf, pl.ds(r, 1), :], gsem.at[buf])

    def scatter_copy(row, r, buf):
        return pltpu.make_async_copy(ybuf.at[buf, pl.ds(r, 1), :], y_hbm.at[pl.ds(row, 1), :], ssem.at[buf])

    def start_gather(ids_ref, buf):
        for r in range(tm):
            gather_copy(ids_ref[0, 0, r], r, buf).start()

    def wait_gather(buf):
        pltpu.make_async_copy(h_hbm.at[pl.ds(0, tm), :], xbuf.at[buf], gsem.at[buf]).wait()

    def start_scatter(buf, n_rows):
        @pl.when(n_rows == tm)
        def _():
            for r in range(tm):
                scatter_copy(dest_ref[0, 0, r], r, buf).start()

        @pl.when(n_rows < tm)
        def _():
            def body(r, c):
                scatter_copy(dest_ref[0, 0, r], r, buf).start()
                return c
            lax.fori_loop(0, n_rows, body, 0)

    def wait_scatter(buf, n_rows):
        @pl.when(n_rows == tm)
        def _():
            pltpu.make_async_copy(ybuf.at[buf], y_hbm.at[pl.ds(0, tm), :], ssem.at[buf]).wait()

        @pl.when(n_rows < tm)
        def _():
            def body(r, c):
                scatter_copy(0, r, buf).wait()
                return c
            lax.fori_loop(0, n_rows, body, 0)

    @pl.when(i == 0)
    def _():
        start_gather(tok_ref, 0)

    @pl.when((i < last) & (rows_next > 0))
    def _():
        start_gather(tok_next_ref, 1 - cur)

    @pl.when(rows > 0)
    def _():
        wait_gather(cur)
        h = xbuf[cur].astype(BF16)
        acc = jnp.zeros((tm, h.shape[1]), F32)
        for c in range(f_dim // tf):
            a = jnp.dot(h, wg_ref[0, :, c * tf:(c + 1) * tf], preferred_element_type=F32)
            b = jnp.dot(h, wu_ref[0, :, c * tf:(c + 1) * tf], preferred_element_type=F32)
            act = ((a * _sigmoid(a)) * b).astype(BF16)
            acc = acc + jnp.dot(act, wd_ref[0, c * tf:(c + 1) * tf, :], preferred_element_type=F32)
        ybuf[cur] = acc
        start_scatter(cur, rows)

    @pl.when((i >= 1) & (rows_prev > 0))
    def _():
        wait_scatter(1 - cur, rows_prev)

    @pl.when((i == last) & (rows > 0))
    def _():
        wait_scatter(cur, rows)


def moe_grouped(h, tile_expert, tile_rows, tok, dest, wg, wu, wd, tm, tf):
    t, d = h.shape
    n_tiles_max = tok.shape[0]
    f_dim = wg.shape[2]
    smem_tile = lambda idx: pl.BlockSpec((1, 1, tm), idx, memory_space=pltpu.SMEM)
    w_in_spec = pl.BlockSpec((1, d, f_dim), lambda i, te, nt: (te[i], 0, 0), pipeline_mode=pl.Buffered(1))
    w_out_spec = pl.BlockSpec((1, f_dim, d), lambda i, te, nt: (te[i], 0, 0), pipeline_mode=pl.Buffered(1))
    grid_spec = pltpu.PrefetchScalarGridSpec(
        num_scalar_prefetch=2,
        grid=(n_tiles_max,),
        in_specs=[
            smem_tile(lambda i, te, nt: (i, 0, 0)),
            smem_tile(lambda i, te, nt: (jnp.minimum(i + 1, n_tiles_max - 1), 0, 0)),
            smem_tile(lambda i, te, nt: (i, 0, 0)),
            pl.BlockSpec(memory_space=pl.ANY),
            w_in_spec, w_in_spec, w_out_spec,
        ],
        out_specs=pl.BlockSpec(memory_space=pl.ANY),
        scratch_shapes=[
            pltpu.VMEM((2, tm, d), F32),
            pltpu.VMEM((2, tm, d), F32),
            pltpu.SemaphoreType.DMA((2,)),
            pltpu.SemaphoreType.DMA((2,)),
        ],
    )
    return pl.pallas_call(
        functools.partial(_moe_group_kernel, tm=tm, tf=tf),
        grid_spec=grid_spec,
        out_shape=jax.ShapeDtypeStruct((2 * t, d), F32),
        compiler_params=pltpu.CompilerParams(dimension_semantics=("arbitrary",),
                                             vmem_limit_bytes=MOE_VMEM_LIMIT_BYTES),
        name="moe_grouped_swiglu",
    )(tile_expert, tile_rows, tok, tok, dest, h, wg, wu, wd)


def _moe_combine_kernel(x_ref, r_ref, y1_ref, y2_ref, o_ref):
    r = r_ref[...]
    o_ref[...] = (x_ref[...] + r[:, ROUTE_W1:ROUTE_W1 + 1] * y1_ref[...]
                  + r[:, ROUTE_W2:ROUTE_W2 + 1] * y2_ref[...])


def moe_combine(x, route, y, tm):
    t, d = x.shape
    nb = t // tm
    return pl.pallas_call(
        _moe_combine_kernel,
        grid=(nb,),
        in_specs=[pl.BlockSpec((tm, d), lambda i: (i, 0)), pl.BlockSpec((tm, LANE), lambda i: (i, 0)),
                  pl.BlockSpec((tm, d), lambda i: (i, 0)), pl.BlockSpec((tm, d), lambda i: (i + nb, 0))],
        out_specs=pl.BlockSpec((tm, d), lambda i: (i, 0)),
        out_shape=jax.ShapeDtypeStruct((t, d), F32),
        compiler_params=_params("parallel"),
        name="moe_combine",
    )(x, route, y, y)


def _pack_in_proj(w):
    o = 0
    z = w[:, o:o + SSD_INNER]; o += SSD_INNER
    xbc = w[:, o:o + SSD_CONV_CH]; o += SSD_CONV_CH
    dt = w[:, o:o + SSD_HEADS]; o += SSD_HEADS
    uv = w[:, o:o + 2 * SGU_WIDTH]; o += 2 * SGU_WIDTH
    q_lat = w[:, o:o + MLA_Q_RANK]; o += MLA_Q_RANK
    kv_lat = w[:, o:o + MLA_KV_RANK]; o += MLA_KV_RANK
    k_pe = w[:, o:o + MLA_ROPE]; o += MLA_ROPE
    gate = w[:, o:]
    d = w.shape[0]
    zeros = lambda n: jnp.zeros((d, n), w.dtype)
    small = jnp.concatenate([
        q_lat, kv_lat,
        zeros(MLA_NOPE), k_pe, zeros(LANE - MLA_QK),
        zeros(MLA_NOPE), _swap_rope_halves(k_pe), zeros(LANE - MLA_QK),
        dt, zeros(LANE - SSD_HEADS)], axis=1)
    return (jnp.concatenate([z, xbc], axis=1).astype(BF16), uv.astype(BF16), gate.astype(BF16),
            small.astype(BF16))


def _pick(n, prefs):
    for p in prefs:
        if n % p == 0:
            return p
    return n


def kernel(x, mix_norm_g, w_in, conv_w, conv_b, dt_bias, a_log, d_skip, ssd_norm_g, sgu_ln_g, sgu_ln_b, sgu_w,
           sgu_b, q_norm_g, w_uq, kv_norm_g, w_ukv, q_head_g, k_head_g, w_br_ssd, w_br_sgu, w_br_mla, w_out,
           ffn_norm_g, ffn_w_gate, ffn_w_up, ffn_w_down, router_w, moe_w_gate, moe_w_up, moe_w_down):
    batch, seq, d = x.shape
    t = batch * seq
    depth = w_in.shape[0]
    tm = _pick(t, (1024, 512, 256, 128))
    xf = x.reshape(t, d)
    for i in range(depth):
        w_zx, w_uv, w_gate, w_small = _pack_in_proj(w_in[i])
        hn = rmsnorm(xf, mix_norm_g[i], tm)
        zx = matmul(hn, w_zx, BF16, tm, _pick(w_zx.shape[1], (1280,)), "in_proj_ssd")
        uv = matmul(hn, w_uv, BF16, tm, _pick(w_uv.shape[1], (1024,)), "in_proj_sgu")
        gate = matmul(hn, w_gate, BF16, tm, _pick(w_gate.shape[1], (1536,)), "in_proj_gate")
        small = matmul(hn, w_small, F32, tm, SMALL_WIDTH, "in_proj_small")
        y_ssd = ssd_branch(zx, small, conv_w[i], conv_b[i], dt_bias[i], a_log[i], d_skip[i], ssd_norm_g[i],
                           batch, seq, _pick(seq // SSD_CHUNK, (2, 1)))
        y_sgu = sgu_branch(uv, sgu_ln_g[i], sgu_ln_b[i], sgu_w[i], sgu_b[i], _pick(seq // SGU_CHUNK, (2, 1)))
        q, k, v = mla_prep(small, q_norm_g[i], w_uq[i], kv_norm_g[i], w_ukv[i], q_head_g[i], k_head_g[i],
                           batch, seq, _pick(seq, (512, 256, 128)))
        y_mla = flash_attention(q, k, v, batch, seq, _pick(seq, (512, 256, 128))).reshape(t, MLA_WIDTH)
        is_moe = i % 2 == 1
        x1, h2 = merge(xf, y_ssd, y_sgu, y_mla, gate, w_br_ssd[i], w_br_sgu[i], w_br_mla[i], w_out[i],
                       ffn_norm_g[i], _pick(t, (512, 256, 128)), F32 if is_moe else BF16)
        j = i // 2
        if not is_moe:
            xf = ffn(h2, x1, ffn_w_gate[j], ffn_w_up[j], ffn_w_down[j], tm, _pick(ffn_w_gate.shape[2], (256,)))
        else:
            route = router(x1, ffn_norm_g[i], router_w[j], tm)
            tm_e = _pick(t, (512, 256, 128))
            tile_expert, tile_rows, tok, dest = _moe_plan(route, t, tm_e)
            y = moe_grouped(h2, tile_expert, tile_rows, tok, dest, moe_w_gate[j].astype(BF16),
                            moe_w_up[j].astype(BF16), moe_w_down[j].astype(BF16), tm_e,
                            _pick(moe_w_gate.shape[3], (512, 256)))
            xf = moe_combine(x1, route, y, tm)
    return xf.reshape(batch, seq, d)
```

```python
import functools
import math

import jax
import jax.numpy as jnp
from jax import lax
from jax.experimental import pallas as pl
from jax.experimental.pallas import tpu as pltpu

F32 = jnp.float32
BF16 = jnp.bfloat16

D_MODEL = 1024
SSD_HEADS = 16
SSD_HEAD_DIM = 64
SSD_INNER = SSD_HEADS * SSD_HEAD_DIM
SSD_GROUPS = 2
SSD_STATE = 128
SSD_CONV = 4
SSD_CHUNK = 128
SSD_CONV_CH = SSD_INNER + 2 * SSD_GROUPS * SSD_STATE
SGU_WIDTH = 1024
SGU_GROUPS = 8
SGU_CHUNK = 128
MLA_HEADS = 16
MLA_Q_RANK = 256
MLA_KV_RANK = 128
MLA_NOPE = 64
MLA_ROPE = 32
MLA_V = 64
MLA_QK = MLA_NOPE + MLA_ROPE
MLA_WIDTH = MLA_HEADS * MLA_V
ROPE_BASE = 10000.0
N_EXPERTS = 8
EPS = 1e-6
LN_EPS = 1e-5

LANE = 128
SUBLANE = 8
VMEM_LIMIT_BYTES = 48 * 1024 * 1024
MOE_VMEM_LIMIT_BYTES = 56 * 1024 * 1024

ROUTE_E1, ROUTE_E2, ROUTE_W1, ROUTE_W2 = 0, 1, 2, 3

SMALL_KPE_TILE = 3
SMALL_KPEP_TILE = 4
SMALL_DT_TILE = 5
SMALL_WIDTH = 6 * LANE


def _params(*semantics):
    return pltpu.CompilerParams(dimension_semantics=semantics, vmem_limit_bytes=VMEM_LIMIT_BYTES)


def _sigmoid(x):
    return 1.0 / (1.0 + jnp.exp(-x))


def _rmsnorm_kernel(x_ref, g_ref, o_ref):
    x = x_ref[...]
    ms = jnp.mean(x * x, axis=-1, keepdims=True)
    o_ref[...] = (x * lax.rsqrt(ms + EPS) * g_ref[...]).astype(o_ref.dtype)


def rmsnorm(x, g, tm):
    t, d = x.shape
    return pl.pallas_call(
        _rmsnorm_kernel,
        grid=(t // tm,),
        in_specs=[pl.BlockSpec((tm, d), lambda i: (i, 0)), pl.BlockSpec((1, d), lambda i: (0, 0))],
        out_specs=pl.BlockSpec((tm, d), lambda i: (i, 0)),
        out_shape=jax.ShapeDtypeStruct((t, d), BF16),
        compiler_params=_params("parallel"),
        name="rmsnorm",
    )(x, g.reshape(1, d))


def _matmul_kernel(a_ref, w_ref, o_ref):
    o_ref[...] = jnp.dot(a_ref[...], w_ref[...], preferred_element_type=F32).astype(o_ref.dtype)


def matmul(a, w, out_dtype, tm, tn, name):
    t, k = a.shape
    n = w.shape[1]
    return pl.pallas_call(
        _matmul_kernel,
        grid=(t // tm, n // tn),
        in_specs=[pl.BlockSpec((tm, k), lambda i, j: (i, 0)), pl.BlockSpec((k, tn), lambda i, j: (0, j))],
        out_specs=pl.BlockSpec((tm, tn), lambda i, j: (i, j)),
        out_shape=jax.ShapeDtypeStruct((t, n), out_dtype),
        compiler_params=_params("parallel", "parallel"),
        name=name,
    )(a, w)


def _ssd_kernel(zx_ref, dt_ref, convw_ref, convb_ref, dtb_ref, alog_ref, dskip_ref, ng_ref, exp_ref,
                o_ref, xpad_ref, st_ref, *, chunks_per_step):
    l = SSD_CHUNK
    half = SSD_INNER // SSD_GROUPS

    @pl.when(pl.program_id(1) == 0)
    def _():
        xpad_ref[...] = jnp.zeros_like(xpad_ref)
        st_ref[...] = jnp.zeros_like(st_ref)

    row = lax.broadcasted_iota(jnp.int32, (l, l), 0)
    lane = lax.broadcasted_iota(jnp.int32, (l, l), 1)
    causal = row >= lane
    low_half = lane < SSD_HEAD_DIM
    tri = causal.astype(F32)
    a_neg = -jnp.exp(alog_ref[...])

    for ci in range(chunks_per_step):
        r0 = ci * l
        z = zx_ref[r0:r0 + l, 0:SSD_INNER].astype(F32)
        xpad_ref[0:SUBLANE, :] = xpad_ref[l:l + SUBLANE, :]
        xpad_ref[SUBLANE:SUBLANE + l, :] = zx_ref[r0:r0 + l, SSD_INNER:SSD_INNER + SSD_CONV_CH].astype(F32)
        conv = convb_ref[...]
        for j in range(SSD_CONV):
            s0 = SUBLANE - (SSD_CONV - 1) + j
            conv = conv + convw_ref[j:j + 1, :] * xpad_ref[s0:s0 + l, :]
        xbc = conv * _sigmoid(conv)
        xh = xbc[:, 0:SSD_INNER]
        bm = xbc[:, SSD_INNER:SSD_INNER + SSD_GROUPS * SSD_STATE].astype(BF16)
        cm = xbc[:, SSD_INNER + SSD_GROUPS * SSD_STATE:].astype(BF16)

        dt_in = dt_ref[r0:r0 + l, :] + dtb_ref[...]
        dt = jnp.maximum(dt_in, 0.0) + jnp.log1p(jnp.exp(-jnp.abs(dt_in)))
        adt = dt * a_neg
        cs = jnp.dot(tri, adt, preferred_element_type=F32, precision=lax.Precision.HIGHEST)
        cs_t = cs.T
        cs_last = cs[l - 1:l, :]
        small = jnp.concatenate([dt, jnp.exp(cs_last - cs), jnp.exp(cs)], axis=0).astype(BF16)
        wide = jnp.dot(small, exp_ref[...], preferred_element_type=F32)
        dt_e = wide[0:l]
        dstate_e = wide[l:2 * l]
        ecs_e = wide[2 * l:3 * l]
        cdec_e = ecs_e[l - 1:l, :]

        xs = xh * dt_e
        xs_b = xs.astype(BF16)
        xd_b = (xs * dstate_e).astype(BF16)

        y_parts = []
        off_parts = []
        for g in range(SSD_GROUPS):
            bg = bm[:, g * SSD_STATE:(g + 1) * SSD_STATE]
            cg = cm[:, g * SSD_STATE:(g + 1) * SSD_STATE]
            cb = lax.dot_general(cg, bg, (((1,), (1,)), ((), ())), preferred_element_type=F32)
            for hp in range(SSD_HEADS // SSD_GROUPS // 2):
                pair = g * (SSD_HEADS // SSD_GROUPS // 2) + hp
                xs_pair = xs_b[:, pair * LANE:(pair + 1) * LANE]
                ys = []
                for h in (2 * pair, 2 * pair + 1):
                    diff = cs[:, h:h + 1] - cs_t[h:h + 1, :]
                    decay = jnp.exp(jnp.where(causal, diff, -jnp.inf))
                    ys.append(jnp.dot((cb * decay).astype(BF16), xs_pair, preferred_element_type=F32))
                y_parts.append(jnp.where(low_half, ys[0], ys[1]))
            st_g = st_ref[:, g * half:(g + 1) * half]
            off_parts.append(jnp.dot(cg, st_g.astype(BF16), preferred_element_type=F32))
            new = lax.dot_general(bg, xd_b[:, g * half:(g + 1) * half], (((0,), (0,)), ((), ())),
                                  preferred_element_type=F32)
            st_ref[:, g * half:(g + 1) * half] = st_g * cdec_e[:, g * half:(g + 1) * half] + new

        y = (jnp.concatenate(y_parts, axis=1) + jnp.concatenate(off_parts, axis=1) * ecs_e
             + xh * dskip_ref[...])
        yg = y * (z * _sigmoid(z))
        outs = []
        for g in range(SSD_GROUPS):
            yh = yg[:, g * half:(g + 1) * half]
            ms = jnp.mean(yh * yh, axis=-1, keepdims=True)
            outs.append(yh * lax.rsqrt(ms + EPS))
        o_ref[r0:r0 + l, :] = (jnp.concatenate(outs, axis=1) * ng_ref[...]).astype(o_ref.dtype)


def ssd_branch(zx, small, conv_w, conv_b, dt_bias, a_log, d_skip, norm_g, batch, seq, chunks_per_step):
    rows = chunks_per_step * SSD_CHUNK
    steps = seq // rows
    pad_h = LANE - SSD_HEADS
    dtb = jnp.pad(dt_bias, (0, pad_h)).reshape(1, LANE)
    alog = jnp.pad(a_log, (0, pad_h)).reshape(1, LANE)
    dskip = jnp.repeat(d_skip, SSD_HEAD_DIM).reshape(1, SSD_INNER)
    expand = (jnp.arange(LANE)[:, None] == jnp.arange(SSD_INNER)[None, :] // SSD_HEAD_DIM).astype(BF16)
    const = lambda b, c: (0, 0)
    return pl.pallas_call(
        functools.partial(_ssd_kernel, chunks_per_step=chunks_per_step),
        grid=(batch, steps),
        in_specs=[
            pl.BlockSpec((rows, SSD_INNER + SSD_CONV_CH), lambda b, c: (b * steps + c, 0)),
            pl.BlockSpec((rows, LANE), lambda b, c: (b * steps + c, SMALL_DT_TILE)),
            pl.BlockSpec((SSD_CONV, SSD_CONV_CH), const),
            pl.BlockSpec((1, SSD_CONV_CH), const),
            pl.BlockSpec((1, LANE), const),
            pl.BlockSpec((1, LANE), const),
            pl.BlockSpec((1, SSD_INNER), const),
            pl.BlockSpec((1, SSD_INNER), const),
            pl.BlockSpec((LANE, SSD_INNER), const),
        ],
        out_specs=pl.BlockSpec((rows, SSD_INNER), lambda b, c: (b * steps + c, 0)),
        out_shape=jax.ShapeDtypeStruct((batch * seq, SSD_INNER), BF16),
        scratch_shapes=[
            pltpu.VMEM((SSD_CHUNK + SUBLANE, SSD_CONV_CH), F32),
            pltpu.VMEM((SSD_STATE, SSD_INNER), F32),
        ],
        compiler_params=_params("parallel", "arbitrary"),
        name="ssd_branch",
    )(zx, small, conv_w, conv_b.reshape(1, -1), dtb, alog, dskip, norm_g.reshape(1, -1), expand)


def _sgu_kernel(uv_ref, g_ref, b_ref, w_ref, bias_ref, o_ref, *, chunks_per_step):
    l = SGU_CHUNK
    uv = uv_ref[...].astype(F32)
    act = 0.5 * uv * (1.0 + lax.erf(uv * (1.0 / math.sqrt(2.0))))
    u = act[:, 0:SGU_WIDTH]
    v = act[:, SGU_WIDTH:]
    mu = jnp.mean(v, axis=-1, keepdims=True)
    vc = v - mu
    var = jnp.mean(vc * vc, axis=-1, keepdims=True)
    vn = (vc * lax.rsqrt(var + LN_EPS) * g_ref[...] + b_ref[...]).astype(BF16)
    row = lax.broadcasted_iota(jnp.int32, (l, l), 0)
    col = lax.broadcasted_iota(jnp.int32, (l, l), 1)
    causal = row >= col
    for g in range(SGU_GROUPS):
        wg = jnp.where(causal, w_ref[g], 0.0).astype(BF16)
        c0 = g * LANE
        for ci in range(chunks_per_step):
            r0 = ci * l
            mixed = jnp.dot(wg, vn[r0:r0 + l, c0:c0 + LANE], preferred_element_type=F32)
            o_ref[r0:r0 + l, c0:c0 + LANE] = (
                u[r0:r0 + l, c0:c0 + LANE] * (mixed + bias_ref[:, c0:c0 + LANE])).astype(o_ref.dtype)


def sgu_branch(uv, ln_g, ln_b, w_s, b_s, chunks_per_step):
    t = uv.shape[0]
    rows = chunks_per_step * SGU_CHUNK
    bias = jnp.repeat(b_s.T, SGU_WIDTH // SGU_GROUPS, axis=1)
    const2 = lambda i: (0, 0)
    return pl.pallas_call(
        functools.partial(_sgu_kernel, chunks_per_step=chunks_per_step),
        grid=(t // rows,),
        in_specs=[
            pl.BlockSpec((rows, 2 * SGU_WIDTH), lambda i: (i, 0)),
            pl.BlockSpec((1, SGU_WIDTH), const2),
            pl.BlockSpec((1, SGU_WIDTH), const2),
            pl.BlockSpec((SGU_GROUPS, SGU_CHUNK, SGU_CHUNK), lambda i: (0, 0, 0)),
            pl.BlockSpec((SGU_CHUNK, SGU_WIDTH), const2),
        ],
        out_specs=pl.BlockSpec((rows, SGU_WIDTH), lambda i: (i, 0)),
        out_shape=jax.ShapeDtypeStruct((t, SGU_WIDTH), BF16),
        compiler_params=_params("parallel"),
        name="sgu_branch",
    )(uv, ln_g.reshape(1, -1), ln_b.reshape(1, -1), w_s, bias)


def _mla_prep_kernel(small_ref, cos_ref, sin_ref, qng_ref, kvng_ref, qhg_ref, khg_ref, wq_ref, wqp_ref, wk_ref,
                     wv_ref, q_ref, k_ref, v_ref):
    sm = small_ref[...]
    q_lat = sm[:, 0:MLA_Q_RANK]
    kv_lat = sm[:, MLA_Q_RANK:MLA_Q_RANK + MLA_KV_RANK]
    kpe = sm[:, SMALL_KPE_TILE * LANE:(SMALL_KPE_TILE + 1) * LANE]
    kpe_p = sm[:, SMALL_KPEP_TILE * LANE:(SMALL_KPEP_TILE + 1) * LANE]

    def rms(x, g):
        return x * lax.rsqrt(jnp.mean(x * x, axis=-1, keepdims=True) + EPS) * g

    ql = rms(q_lat, qng_ref[...]).astype(BF16)
    kvl = rms(kv_lat, kvng_ref[...]).astype(BF16)
    v_ref[...] = jnp.dot(kvl, wv_ref[...], preferred_element_type=F32).astype(v_ref.dtype)

    ones_blk = jnp.ones((LANE, LANE), BF16)

    def head_inv_rms(x):
        sq = (x * x).astype(BF16)
        ssq = jnp.concatenate(
            [jnp.dot(sq[:, h * LANE:(h + 1) * LANE], ones_blk, preferred_element_type=F32)
             for h in range(MLA_HEADS)], axis=1)
        return lax.rsqrt(ssq * (1.0 / MLA_QK) + EPS)

    cos = cos_ref[...]
    sin = sin_ref[...]
    rep = lambda a: jnp.tile(a, (1, MLA_HEADS))

    q_scale = MLA_QK ** -0.5 * math.log2(math.e)
    q_x = jnp.dot(ql, wq_ref[...], preferred_element_type=F32)
    q_xp = jnp.dot(ql, wqp_ref[...], preferred_element_type=F32)
    q_a = cos * qhg_ref[0:1, :] * q_scale
    q_b = sin * qhg_ref[1:2, :] * q_scale
    q_out = head_inv_rms(q_x) * (q_x * rep(q_a) + q_xp * rep(q_b))
    k_x = jnp.dot(kvl, wk_ref[...], preferred_element_type=F32) + rep(kpe)
    k_a = cos * khg_ref[0:1, :]
    k_b = sin * khg_ref[1:2, :]
    k_out = head_inv_rms(k_x) * (k_x * rep(k_a) + rep(kpe_p * k_b))
    for h in range(MLA_HEADS):
        q_ref[0, h] = q_out[:, h * LANE:(h + 1) * LANE].astype(q_ref.dtype)
        k_ref[0, h] = k_out[:, h * LANE:(h + 1) * LANE].astype(k_ref.dtype)


def _swap_rope_halves(a):
    half = MLA_ROPE // 2
    return jnp.concatenate([a[..., half:], a[..., :half]], axis=-1)


def _head_gain_rows(g):
    direct = jnp.pad(g, (0, LANE - MLA_QK))
    partner = jnp.pad(_swap_rope_halves(g[MLA_NOPE:]), (MLA_NOPE, LANE - MLA_QK))
    return jnp.stack([direct, partner])


def mla_prep(small, q_norm_g, w_uq, kv_norm_g, w_ukv, q_head_g, k_head_g, batch, seq, rows):
    steps = seq // rows
    w_uq3 = w_uq.reshape(MLA_Q_RANK, MLA_HEADS, MLA_QK)
    wq = jnp.pad(w_uq3, ((0, 0), (0, 0), (0, LANE - MLA_QK))).reshape(MLA_Q_RANK, -1).astype(BF16)
    wqp = jnp.pad(_swap_rope_halves(w_uq3[:, :, MLA_NOPE:]), ((0, 0), (0, 0), (MLA_NOPE, LANE - MLA_QK)))
    wqp = wqp.reshape(MLA_Q_RANK, -1).astype(BF16)
    w_ukv3 = w_ukv.reshape(MLA_KV_RANK, MLA_HEADS, MLA_NOPE + MLA_V)
    wk = jnp.pad(w_ukv3[:, :, :MLA_NOPE], ((0, 0), (0, 0), (0, LANE - MLA_NOPE))).reshape(MLA_KV_RANK, -1).astype(BF16)
    wv = w_ukv3[:, :, MLA_NOPE:].reshape(MLA_KV_RANK, MLA_WIDTH).astype(BF16)
    qhg = _head_gain_rows(q_head_g)
    khg = _head_gain_rows(k_head_g)
    inv = 1.0 / (ROPE_BASE ** (jnp.arange(0, MLA_ROPE, 2, dtype=F32) / MLA_ROPE))
    ang = jnp.arange(seq, dtype=F32)[:, None] * inv[None, :]
    c, s = jnp.cos(ang), jnp.sin(ang)
    ones = jnp.ones((seq, MLA_NOPE), F32)
    zeros = jnp.zeros((seq, MLA_NOPE), F32)
    tail1 = jnp.ones((seq, LANE - MLA_QK), F32)
    tail0 = jnp.zeros((seq, LANE - MLA_QK), F32)
    cos_t = jnp.concatenate([ones, c, c, tail1], axis=1)
    sin_t = jnp.concatenate([zeros, -s, s, tail0], axis=1)
    const = lambda b, i: (0, 0)
    qk_shape = jax.ShapeDtypeStruct((batch, MLA_HEADS, seq, LANE), BF16)
    return pl.pallas_call(
        _mla_prep_kernel,
        grid=(batch, steps),
        in_specs=[
            pl.BlockSpec((rows, SMALL_WIDTH), lambda b, i: (b * steps + i, 0)),
            pl.BlockSpec((rows, LANE), lambda b, i: (i, 0)),
            pl.BlockSpec((rows, LANE), lambda b, i: (i, 0)),
            pl.BlockSpec((1, MLA_Q_RANK), const),
            pl.BlockSpec((1, MLA_KV_RANK), const),
            pl.BlockSpec((2, LANE), const),
            pl.BlockSpec((2, LANE), const),
            pl.BlockSpec((MLA_Q_RANK, MLA_HEADS * LANE), const),
            pl.BlockSpec((MLA_Q_RANK, MLA_HEADS * LANE), const),
            pl.BlockSpec((MLA_KV_RANK, MLA_HEADS * LANE), const),
            pl.BlockSpec((MLA_KV_RANK, MLA_WIDTH), const),
        ],
        out_specs=[
            pl.BlockSpec((1, MLA_HEADS, rows, LANE), lambda b, i: (b, 0, i, 0)),
            pl.BlockSpec((1, MLA_HEADS, rows, LANE), lambda b, i: (b, 0, i, 0)),
            pl.BlockSpec((rows, MLA_WIDTH), lambda b, i: (b * steps + i, 0)),
        ],
        out_shape=[qk_shape, qk_shape, jax.ShapeDtypeStruct((batch * seq, MLA_WIDTH), BF16)],
        compiler_params=_params("parallel", "parallel"),
        name="mla_prep",
    )(small, cos_t, sin_t, q_norm_g.reshape(1, -1), kv_norm_g.reshape(1, -1), qhg, khg, wq, wqp, wk, wv)


FLASH_HEADS_PER_STEP = 4
FLASH_TQ = (1024, 512, 256, 128)
FLASH_TK = (512, 256, 128)


def _flash_kernel(qi_ref, kj_ref, q_ref, k_ref, v_ref, o_ref, m_ref, acc_ref, *, tq, tk):
    t = pl.program_id(2)
    qi = qi_ref[t]
    kj = kj_ref[t]
    heads = FLASH_HEADS_PER_STEP
    band = kj * tk - qi * tq
    is_last = band + tk == tq

    @pl.when(kj == 0)
    def _():
        m_ref[...] = jnp.full_like(m_ref, -jnp.inf)
        acc_ref[...] = jnp.zeros_like(acc_ref)

    def step(lo, hi, triangular):
        rows = hi - lo
        first = lax.broadcasted_iota(jnp.int32, (tk, LANE), 1) < MLA_V
        one = jnp.ones((tk, LANE), BF16)
        if triangular:
            visible = (lax.broadcasted_iota(jnp.int32, (rows, tk), 1)
                       <= lax.broadcasted_iota(jnp.int32, (rows, tk), 0))
        for hh in range(heads):
            v = v_ref[0, :, (hh // 2) * LANE:(hh // 2 + 1) * LANE]
            v_ones = jnp.where(first, v, one) if hh % 2 == 0 else jnp.where(first, one, v)
            s = lax.dot_general(q_ref[0, hh, lo:hi, :], k_ref[0, hh], (((1,), (1,)), ((), ())),
                                preferred_element_type=F32)
            if triangular:
                s = jnp.where(visible, s, -jnp.inf)
            m_prev = m_ref[hh, lo:hi, :]
            m_new = jnp.maximum(m_prev, jnp.max(s, axis=1, keepdims=True))
            p = jnp.exp2(s - jnp.tile(m_new, (1, tk // LANE)))
            alpha = jnp.exp2(m_prev - m_new)
            acc_ref[hh, lo:hi, :] = (alpha * acc_ref[hh, lo:hi, :]
                                     + jnp.dot(p.astype(BF16), v_ones, preferred_element_type=F32))
            m_ref[hh, lo:hi, :] = m_new

    @pl.when(band < 0)
    def _():
        step(0, tq, False)

    for j in range(tq // tk):
        @pl.when(band == j * tk)
        def _(j=j):
            step(j * tk, (j + 1) * tk, True)
            if (j + 1) * tk < tq:
                step((j + 1) * tk, tq, False)

    @pl.when(is_last)
    def _():
        lane = lax.broadcasted_iota(jnp.int32, (tq, LANE), 1)
        for pair in range(heads // 2):
            a0 = acc_ref[2 * pair]
            a1 = acc_ref[2 * pair + 1]
            out = jnp.where(lane < MLA_V, a0 / pltpu.roll(a0, MLA_V, 1), a1 / pltpu.roll(a1, MLA_V, 1))
            o_ref[0, :, pair * LANE:(pair + 1) * LANE] = out.astype(o_ref.dtype)


def flash_attention(q, k, v, batch, seq, tq, tk):
    assert tq % tk == 0 and seq % tq == 0
    heads = FLASH_HEADS_PER_STEP
    width = heads * MLA_V
    pairs = [(i, j) for i in range(seq // tq) for j in range((i + 1) * tq // tk)]
    qi = jnp.asarray([p[0] for p in pairs], jnp.int32)
    kj = jnp.asarray([p[1] for p in pairs], jnp.int32)
    grid_spec = pltpu.PrefetchScalarGridSpec(
        num_scalar_prefetch=2,
        grid=(batch, MLA_HEADS // heads, len(pairs)),
        in_specs=[
            pl.BlockSpec((1, heads, tq, LANE), lambda b, h, t, qi, kj: (b, h, qi[t], 0)),
            pl.BlockSpec((1, heads, tk, LANE), lambda b, h, t, qi, kj: (b, h, kj[t], 0)),
            pl.BlockSpec((1, tk, width), lambda b, h, t, qi, kj: (b, kj[t], h)),
        ],
        out_specs=pl.BlockSpec((1, tq, width), lambda b, h, t, qi, kj: (b, qi[t], h)),
        scratch_shapes=[
            pltpu.VMEM((heads, tq, LANE), F32),
            pltpu.VMEM((heads, tq, LANE), F32),
        ],
    )
    return pl.pallas_call(
        functools.partial(_flash_kernel, tq=tq, tk=tk),
        grid_spec=grid_spec,
        out_shape=jax.ShapeDtypeStruct((batch, seq, MLA_WIDTH), BF16),
        compiler_params=_params("parallel", "parallel", "arbitrary"),
        name="mla_flash_attention",
    )(qi, kj, q, k, v.reshape(batch, seq, MLA_WIDTH))


def _merge_kernel(x_ref, ys_ref, yg_ref, ym_ref, gate_ref, ws_ref, wg_ref, wm_ref, wo_ref, g2_ref,
                  x1_ref, h2_ref):
    d = D_MODEL
    gate = _sigmoid(gate_ref[...].astype(F32))
    merged = (gate[:, 0:d] * jnp.dot(ys_ref[...], ws_ref[...], preferred_element_type=F32)
              + gate[:, d:2 * d] * jnp.dot(yg_ref[...], wg_ref[...], preferred_element_type=F32)
              + gate[:, 2 * d:] * jnp.dot(ym_ref[...], wm_ref[...], preferred_element_type=F32))
    x1 = x_ref[...] + jnp.dot(merged.astype(BF16), wo_ref[...], preferred_element_type=F32)
    x1_ref[...] = x1
    ms = jnp.mean(x1 * x1, axis=-1, keepdims=True)
    h2_ref[...] = (x1 * lax.rsqrt(ms + EPS) * g2_ref[...]).astype(h2_ref.dtype)


def merge(x, y_ssd, y_sgu, y_mla, gate, w_ssd, w_sgu, w_mla, w_out, ffn_norm_g, tm, h2_dtype):
    t, d = x.shape
    row = lambda i: (i, 0)
    const = lambda i: (0, 0)
    act = pl.BlockSpec((tm, d), row)
    wspec = pl.BlockSpec((d, d), const)
    return pl.pallas_call(
        _merge_kernel,
        grid=(t // tm,),
        in_specs=[act, act, act, act, pl.BlockSpec((tm, 3 * d), row), wspec, wspec, wspec, wspec,
                  pl.BlockSpec((1, d), const)],
        out_specs=[act, act],
        out_shape=[jax.ShapeDtypeStruct((t, d), F32), jax.ShapeDtypeStruct((t, d), h2_dtype)],
        compiler_params=_params("parallel"),
        name="merge_out_proj",
    )(x, y_ssd, y_sgu, y_mla, gate, w_ssd.astype(BF16), w_sgu.astype(BF16), w_mla.astype(BF16),
      w_out.astype(BF16), ffn_norm_g.reshape(1, d))


def _ffn_kernel(h_ref, x_ref, wg_ref, wu_ref, wd_ref, o_ref, *, tf):
    h = h_ref[...]
    acc = x_ref[...]
    for c in range(wg_ref.shape[1] // tf):
        a = jnp.dot(h, wg_ref[:, c * tf:(c + 1) * tf], preferred_element_type=F32)
        b = jnp.dot(h, wu_ref[:, c * tf:(c + 1) * tf], preferred_element_type=F32)
        act = ((a * _sigmoid(a)) * b).astype(BF16)
        acc = acc + jnp.dot(act, wd_ref[c * tf:(c + 1) * tf, :], preferred_element_type=F32)
    o_ref[...] = acc


def ffn(h, x, wg, wu, wd, tm, tf):
    t, d = x.shape
    f_dim = wg.shape[1]
    row = lambda i: (i, 0)
    const = lambda i: (0, 0)
    return pl.pallas_call(
        functools.partial(_ffn_kernel, tf=tf),
        grid=(t // tm,),
        in_specs=[
            pl.BlockSpec((tm, d), row),
            pl.BlockSpec((tm, d), row),
            pl.BlockSpec((d, f_dim), const),
            pl.BlockSpec((d, f_dim), const),
            pl.BlockSpec((f_dim, d), const),
        ],
        out_specs=pl.BlockSpec((tm, d), row),
        out_shape=jax.ShapeDtypeStruct((t, d), F32),
        compiler_params=_params("parallel"),
        name="swiglu_ffn",
    )(h, x, wg.astype(BF16), wu.astype(BF16), wd.astype(BF16))


def _router_kernel(x_ref, g_ref, rw_ref, c_ref):
    x = x_ref[...]
    h = x * lax.rsqrt(jnp.mean(x * x, axis=-1, keepdims=True) + EPS) * g_ref[...]
    logits = jnp.dot(h, rw_ref[...], preferred_element_type=F32, precision=lax.Precision.HIGHEST)
    lane = lax.broadcasted_iota(jnp.int32, logits.shape, 1)
    lg = jnp.where(lane < N_EXPERTS, logits, -jnp.inf)
    m1 = jnp.max(lg, axis=-1, keepdims=True)
    i1 = jnp.min(jnp.where(lg == m1, lane, LANE), axis=-1, keepdims=True)
    lg2 = jnp.where(lane == i1, -jnp.inf, lg)
    m2 = jnp.max(lg2, axis=-1, keepdims=True)
    i2 = jnp.min(jnp.where(lg2 == m2, lane, LANE), axis=-1, keepdims=True)
    e2 = jnp.exp(m2 - m1)
    w1 = 1.0 / (1.0 + e2)
    w2 = e2 / (1.0 + e2)
    c_ref[...] = (jnp.where(lane == ROUTE_E1, i1.astype(F32), 0.0) + jnp.where(lane == ROUTE_E2, i2.astype(F32), 0.0)
                  + jnp.where(lane == ROUTE_W1, w1, 0.0) + jnp.where(lane == ROUTE_W2, w2, 0.0))


def router(x, g, router_w, tm):
    t, d = x.shape
    rw = jnp.pad(router_w, ((0, 0), (0, LANE - N_EXPERTS)))
    return pl.pallas_call(
        _router_kernel,
        grid=(t // tm,),
        in_specs=[pl.BlockSpec((tm, d), lambda i: (i, 0)), pl.BlockSpec((1, d), lambda i: (0, 0)),
                  pl.BlockSpec((d, LANE), lambda i: (0, 0))],
        out_specs=pl.BlockSpec((tm, LANE), lambda i: (i, 0)),
        out_shape=jax.ShapeDtypeStruct((t, LANE), F32),
        compiler_params=_params("parallel"),
        name="moe_router",
    )(x, g.reshape(1, d), rw)


def _moe_plan(route, t, tm):
    n_assign = 2 * t
    n_tiles_max = n_assign // tm + N_EXPERTS
    e = jnp.concatenate([route[:, ROUTE_E1], route[:, ROUTE_E2]]).astype(jnp.int32)
    onehot = (e[:, None] == jnp.arange(N_EXPERTS, dtype=jnp.int32)[None, :]).astype(jnp.int32)
    csum = jnp.cumsum(onehot, axis=0)
    rank = jnp.sum(csum * onehot, axis=1) - 1
    counts = csum[-1]
    tiles_per = (counts + tm - 1) // tm
    tile_end = jnp.cumsum(tiles_per)
    n_tiles = tile_end[-1]
    pos = jnp.sum(onehot * (tile_end - tiles_per)[None, :], axis=1) * tm + rank
    tile = jnp.minimum(jnp.arange(n_tiles_max, dtype=jnp.int32), n_tiles - 1)
    tile_expert = jnp.sum((tile[:, None] >= tile_end[None, :]).astype(jnp.int32), axis=1)
    shape3 = (t // tm, 1, tm)
    return (tile_expert, tile.astype(jnp.int32), n_tiles.reshape(1).astype(jnp.int32),
            pos[:t].reshape(shape3), pos[t:].reshape(shape3))


def _moe_dispatch_kernel(pos1_ref, pos2_ref, h_ref, init_hbm, xs_hbm, stage, sem, *, tm):
    del init_hbm
    i = pl.program_id(0)
    last = pl.num_programs(0) - 1
    cur = i % 2
    stage[cur] = h_ref[...]

    def row_copy(r, slot, parity):
        return pltpu.make_async_copy(stage.at[parity, pl.ds(r, 1), :], xs_hbm.at[pl.ds(slot, 1), :], sem.at[parity])

    def drain(parity):
        for _ in range(2):
            pltpu.make_async_copy(stage.at[parity], xs_hbm.at[pl.ds(0, tm), :], sem.at[parity]).wait()

    for r in range(tm):
        row_copy(r, pos1_ref[0, 0, r], cur).start()
        row_copy(r, pos2_ref[0, 0, r], cur).start()

    @pl.when(i >= 1)
    def _():
        drain(1 - cur)

    @pl.when(i == last)
    def _():
        drain(cur)


def moe_dispatch(h, pos1, pos2, n_slots, tm):
    t, d = h.shape
    smem_tile = pl.BlockSpec((1, 1, tm), lambda i: (i, 0, 0), memory_space=pltpu.SMEM)
    return pl.pallas_call(
        functools.partial(_moe_dispatch_kernel, tm=tm),
        grid=(t // tm,),
        in_specs=[smem_tile, smem_tile, pl.BlockSpec((tm, d), lambda i: (i, 0)), pl.BlockSpec(memory_space=pl.ANY)],
        out_specs=pl.BlockSpec(memory_space=pl.ANY),
        out_shape=jax.ShapeDtypeStruct((n_slots, d), h.dtype),
        scratch_shapes=[pltpu.VMEM((2, tm, d), h.dtype), pltpu.SemaphoreType.DMA((2,))],
        input_output_aliases={3: 0},
        compiler_params=_params("arbitrary"),
        name="moe_dispatch",
    )(pos1, pos2, h, jnp.zeros((n_slots, d), h.dtype))


def _moe_group_kernel(texp_ref, tile_ref, ntiles_ref, x_ref, wg_ref, wu_ref, wd_ref, y_ref, *, tf):
    del texp_ref, tile_ref
    f_dim = wg_ref.shape[2]
    used = pl.program_id(0) < ntiles_ref[0]

    @pl.when(jnp.logical_not(used))
    def _():
        y_ref[...] = jnp.zeros_like(y_ref)

    @pl.when(used)
    def _():
        h = x_ref[...].astype(BF16)
        acc = jnp.zeros(y_ref.shape, F32)
        for c in range(f_dim // tf):
            a = jnp.dot(h, wg_ref[0, :, c * tf:(c + 1) * tf], preferred_element_type=F32)
            b = jnp.dot(h, wu_ref[0, :, c * tf:(c + 1) * tf], preferred_element_type=F32)
            act = ((a * _sigmoid(a)) * b).astype(BF16)
            acc = acc + jnp.dot(act, wd_ref[0, c * tf:(c + 1) * tf, :], preferred_element_type=F32)
        y_ref[...] = acc


def moe_grouped(xs, tile_expert, tile_block, n_tiles, wg, wu, wd, tm, tf):
    n_slots, d = xs.shape
    f_dim = wg.shape[2]
    row_spec = pl.BlockSpec((tm, d), lambda i, te, tb, nt: (tb[i], 0))
    w_in_spec = pl.BlockSpec((1, d, f_dim), lambda i, te, tb, nt: (te[i], 0, 0), pipeline_mode=pl.Buffered(1))
    w_out_spec = pl.BlockSpec((1, f_dim, d), lambda i, te, tb, nt: (te[i], 0, 0), pipeline_mode=pl.Buffered(1))
    grid_spec = pltpu.PrefetchScalarGridSpec(
        num_scalar_prefetch=3,
        grid=(n_slots // tm,),
        in_specs=[row_spec, w_in_spec, w_in_spec, w_out_spec],
        out_specs=pl.BlockSpec((tm, d), lambda i, te, tb, nt: (i, 0)),
    )
    return pl.pallas_call(
        functools.partial(_moe_group_kernel, tf=tf),
        grid_spec=grid_spec,
        out_shape=jax.ShapeDtypeStruct((n_slots, d), F32),
        compiler_params=pltpu.CompilerParams(dimension_semantics=("arbitrary",),
                                             vmem_limit_bytes=MOE_VMEM_LIMIT_BYTES),
        name="moe_grouped_swiglu",
    )(tile_expert, tile_block, n_tiles, xs, wg, wu, wd)


def _moe_combine_kernel(pos1_ref, pos2_ref, pos1_next_ref, pos2_next_ref, x_ref, r_ref, y_hbm, o_ref, gbuf, sem,
                        *, tm):
    i = pl.program_id(0)
    last = pl.num_programs(0) - 1
    cur = i % 2

    def start_gather(p1_ref, p2_ref, buf):
        for r in range(tm):
            pltpu.make_async_copy(y_hbm.at[pl.ds(p1_ref[0, 0, r], 1), :], gbuf.at[buf, pl.ds(r, 1), :],
                                  sem.at[buf]).start()
            pltpu.make_async_copy(y_hbm.at[pl.ds(p2_ref[0, 0, r], 1), :], gbuf.at[buf, pl.ds(tm + r, 1), :],
                                  sem.at[buf]).start()

    @pl.when(i == 0)
    def _():
        start_gather(pos1_ref, pos2_ref, 0)

    @pl.when(i < last)
    def _():
        start_gather(pos1_next_ref, pos2_next_ref, 1 - cur)

    pltpu.make_async_copy(y_hbm.at[pl.ds(0, 2 * tm), :], gbuf.at[cur], sem.at[cur]).wait()
    r = r_ref[...]
    o_ref[...] = (x_ref[...] + r[:, ROUTE_W1:ROUTE_W1 + 1] * gbuf[cur, 0:tm, :]
                  + r[:, ROUTE_W2:ROUTE_W2 + 1] * gbuf[cur, tm:2 * tm, :])


def moe_combine(x, route, y, pos1, pos2, tm):
    t, d = x.shape
    nb = t // tm
    smem_cur = pl.BlockSpec((1, 1, tm), lambda i: (i, 0, 0), memory_space=pltpu.SMEM)
    smem_next = pl.BlockSpec((1, 1, tm), lambda i: (jnp.minimum(i + 1, nb - 1), 0, 0), memory_space=pltpu.SMEM)
    return pl.pallas_call(
        functools.partial(_moe_combine_kernel, tm=tm),
        grid=(nb,),
        in_specs=[smem_cur, smem_cur, smem_next, smem_next,
                  pl.BlockSpec((tm, d), lambda i: (i, 0)), pl.BlockSpec((tm, LANE), lambda i: (i, 0)),
                  pl.BlockSpec(memory_space=pl.ANY)],
        out_specs=pl.BlockSpec((tm, d), lambda i: (i, 0)),
        out_shape=jax.ShapeDtypeStruct((t, d), F32),
        scratch_shapes=[pltpu.VMEM((2, 2 * tm, d), F32), pltpu.SemaphoreType.DMA((2,))],
        compiler_params=_params("arbitrary"),
        name="moe_combine",
    )(pos1, pos2, pos1, pos2, x, route, y)


def _pack_in_proj(w):
    o = 0
    z = w[:, o:o + SSD_INNER]; o += SSD_INNER
    xbc = w[:, o:o + SSD_CONV_CH]; o += SSD_CONV_CH
    dt = w[:, o:o + SSD_HEADS]; o += SSD_HEADS
    uv = w[:, o:o + 2 * SGU_WIDTH]; o += 2 * SGU_WIDTH
    q_lat = w[:, o:o + MLA_Q_RANK]; o += MLA_Q_RANK
    kv_lat = w[:, o:o + MLA_KV_RANK]; o += MLA_KV_RANK
    k_pe = w[:, o:o + MLA_ROPE]; o += MLA_ROPE
    gate = w[:, o:]
    d = w.shape[0]
    zeros = lambda n: jnp.zeros((d, n), w.dtype)
    small = jnp.concatenate([
        q_lat, kv_lat,
        zeros(MLA_NOPE), k_pe, zeros(LANE - MLA_QK),
        zeros(MLA_NOPE), _swap_rope_halves(k_pe), zeros(LANE - MLA_QK),
        dt, zeros(LANE - SSD_HEADS)], axis=1)
    return (jnp.concatenate([z, xbc], axis=1).astype(BF16), uv.astype(BF16), gate.astype(BF16),
            small.astype(BF16))


def _pick(n, prefs):
    for p in prefs:
        if n % p == 0:
            return p
    return n


def kernel(x, mix_norm_g, w_in, conv_w, conv_b, dt_bias, a_log, d_skip, ssd_norm_g, sgu_ln_g, sgu_ln_b, sgu_w,
           sgu_b, q_norm_g, w_uq, kv_norm_g, w_ukv, q_head_g, k_head_g, w_br_ssd, w_br_sgu, w_br_mla, w_out,
           ffn_norm_g, ffn_w_gate, ffn_w_up, ffn_w_down, router_w, moe_w_gate, moe_w_up, moe_w_down):
    batch, seq, d = x.shape
    t = batch * seq
    depth = w_in.shape[0]
    tm = _pick(t, (1024, 512, 256, 128))
    xf = x.reshape(t, d)
    for i in range(depth):
        w_zx, w_uv, w_gate, w_small = _pack_in_proj(w_in[i])
        hn = rmsnorm(xf, mix_norm_g[i], tm)
        zx = matmul(hn, w_zx, BF16, tm, _pick(w_zx.shape[1], (1280,)), "in_proj_ssd")
        uv = matmul(hn, w_uv, BF16, tm, _pick(w_uv.shape[1], (1024,)), "in_proj_sgu")
        gate = matmul(hn, w_gate, BF16, tm, _pick(w_gate.shape[1], (1536,)), "in_proj_gate")
        small = matmul(hn, w_small, F32, tm, SMALL_WIDTH, "in_proj_small")
        y_ssd = ssd_branch(zx, small, conv_w[i], conv_b[i], dt_bias[i], a_log[i], d_skip[i], ssd_norm_g[i],
                           batch, seq, _pick(seq // SSD_CHUNK, (2, 1)))
        y_sgu = sgu_branch(uv, sgu_ln_g[i], sgu_ln_b[i], sgu_w[i], sgu_b[i], _pick(seq // SGU_CHUNK, (2, 1)))
        q, k, v = mla_prep(small, q_norm_g[i], w_uq[i], kv_norm_g[i], w_ukv[i], q_head_g[i], k_head_g[i],
                           batch, seq, _pick(seq, (512, 256, 128)))
        y_mla = flash_attention(q, k, v, batch, seq, _pick(seq, FLASH_TQ), _pick(seq, FLASH_TK)).reshape(t, MLA_WIDTH)
        is_moe = i % 2 == 1
        x1, h2 = merge(xf, y_ssd, y_sgu, y_mla, gate, w_br_ssd[i], w_br_sgu[i], w_br_mla[i], w_out[i],
                       ffn_norm_g[i], _pick(t, (512, 256, 128)), F32 if is_moe else BF16)
        j = i // 2
        if not is_moe:
            xf = ffn(h2, x1, ffn_w_gate[j], ffn_w_up[j], ffn_w_down[j], _pick(t, (512, 256, 128)),
                     _pick(ffn_w_gate.shape[2], (256, 128)))
        else:
            route = router(x1, ffn_norm_g[i], router_w[j], tm)
            tm_e = _pick(t, (512, 256, 128))
            tile_expert, tile_block, n_tiles, pos1, pos2 = _moe_plan(route, t, tm_e)
            xs = moe_dispatch(h2, pos1, pos2, (2 * t // tm_e + N_EXPERTS) * tm_e, tm_e)
            ys = moe_grouped(xs, tile_expert, tile_block, n_tiles, moe_w_gate[j].astype(BF16),
                             moe_w_up[j].astype(BF16), moe_w_down[j].astype(BF16), tm_e,
                             _pick(moe_w_gate.shape[3], (512, 256)))
            xf = moe_combine(x1, route, ys, pos1, pos2, tm_e)
    return xf.reshape(batch, seq, d)
```

```python
import functools
import math

import jax
import jax.numpy as jnp
from jax import lax
from jax.experimental import pallas as pl
from jax.experimental.pallas import tpu as pltpu

F32 = jnp.float32
BF16 = jnp.bfloat16

D_MODEL = 1024
SSD_HEADS = 16
SSD_HEAD_DIM = 64
SSD_INNER = SSD_HEADS * SSD_HEAD_DIM
SSD_GROUPS = 2
SSD_STATE = 128
SSD_CONV = 4
SSD_CHUNK = 128
SSD_CONV_CH = SSD_INNER + 2 * SSD_GROUPS * SSD_STATE
SGU_WIDTH = 1024
SGU_GROUPS = 8
SGU_CHUNK = 128
MLA_HEADS = 16
MLA_Q_RANK = 256
MLA_KV_RANK = 128
MLA_NOPE = 64
MLA_ROPE = 32
MLA_V = 64
MLA_QK = MLA_NOPE + MLA_ROPE
MLA_WIDTH = MLA_HEADS * MLA_V
ROPE_BASE = 10000.0
N_EXPERTS = 8
EPS = 1e-6
LN_EPS = 1e-5

LANE = 128
SUBLANE = 8
VMEM_LIMIT_BYTES = 48 * 1024 * 1024
MOE_VMEM_LIMIT_BYTES = 56 * 1024 * 1024

ROUTE_E1, ROUTE_E2, ROUTE_W1, ROUTE_W2 = 0, 1, 2, 3

SMALL_KPE_TILE = 3
SMALL_KPEP_TILE = 4
SMALL_DT_TILE = 5
SMALL_WIDTH = 6 * LANE


def _params(*semantics):
    return pltpu.CompilerParams(dimension_semantics=semantics, vmem_limit_bytes=VMEM_LIMIT_BYTES)


def _sigmoid(x):
    return 1.0 / (1.0 + jnp.exp(-x))


def _rmsnorm_kernel(x_ref, g_ref, o_ref):
    x = x_ref[...]
    ms = jnp.mean(x * x, axis=-1, keepdims=True)
    o_ref[...] = (x * lax.rsqrt(ms + EPS) * g_ref[...]).astype(o_ref.dtype)


def rmsnorm(x, g, tm):
    t, d = x.shape
    return pl.pallas_call(
        _rmsnorm_kernel,
        grid=(t // tm,),
        in_specs=[pl.BlockSpec((tm, d), lambda i: (i, 0)), pl.BlockSpec((1, d), lambda i: (0, 0))],
        out_specs=pl.BlockSpec((tm, d), lambda i: (i, 0)),
        out_shape=jax.ShapeDtypeStruct((t, d), BF16),
        compiler_params=_params("parallel"),
        name="rmsnorm",
    )(x, g.reshape(1, d))


def _matmul_kernel(a_ref, w_ref, o_ref):
    o_ref[...] = jnp.dot(a_ref[...], w_ref[...], preferred_element_type=F32).astype(o_ref.dtype)


def matmul(a, w, out_dtype, tm, tn, name):
    t, k = a.shape
    n = w.shape[1]
    return pl.pallas_call(
        _matmul_kernel,
        grid=(t // tm, n // tn),
        in_specs=[pl.BlockSpec((tm, k), lambda i, j: (i, 0)), pl.BlockSpec((k, tn), lambda i, j: (0, j))],
        out_specs=pl.BlockSpec((tm, tn), lambda i, j: (i, j)),
        out_shape=jax.ShapeDtypeStruct((t, n), out_dtype),
        compiler_params=_params("parallel", "parallel"),
        name=name,
    )(a, w)


def _ssd_kernel(zx_ref, dt_ref, convw_ref, convb_ref, dtb_ref, alog_ref, dskip_ref, ng_ref, exp_ref,
                o_ref, xpad_ref, st_ref, *, chunks_per_step):
    l = SSD_CHUNK
    half = SSD_INNER // SSD_GROUPS

    @pl.when(pl.program_id(1) == 0)
    def _():
        xpad_ref[...] = jnp.zeros_like(xpad_ref)
        st_ref[...] = jnp.zeros_like(st_ref)

    row = lax.broadcasted_iota(jnp.int32, (l, l), 0)
    lane = lax.broadcasted_iota(jnp.int32, (l, l), 1)
    causal = row >= lane
    low_half = lane < SSD_HEAD_DIM
    tri = causal.astype(F32)
    a_neg = -jnp.exp(alog_ref[...])

    for ci in range(chunks_per_step):
        r0 = ci * l
        z = zx_ref[r0:r0 + l, 0:SSD_INNER].astype(F32)
        xpad_ref[0:SUBLANE, :] = xpad_ref[l:l + SUBLANE, :]
        xpad_ref[SUBLANE:SUBLANE + l, :] = zx_ref[r0:r0 + l, SSD_INNER:SSD_INNER + SSD_CONV_CH].astype(F32)
        conv = convb_ref[...]
        for j in range(SSD_CONV):
            s0 = SUBLANE - (SSD_CONV - 1) + j
            conv = conv + convw_ref[j:j + 1, :] * xpad_ref[s0:s0 + l, :]
        xbc = conv * _sigmoid(conv)
        xh = xbc[:, 0:SSD_INNER]
        bm = xbc[:, SSD_INNER:SSD_INNER + SSD_GROUPS * SSD_STATE].astype(BF16)
        cm = xbc[:, SSD_INNER + SSD_GROUPS * SSD_STATE:].astype(BF16)

        dt_in = dt_ref[r0:r0 + l, :] + dtb_ref[...]
        dt = jnp.maximum(dt_in, 0.0) + jnp.log1p(jnp.exp(-jnp.abs(dt_in)))
        adt = dt * a_neg
        cs = jnp.dot(tri, adt, preferred_element_type=F32, precision=lax.Precision.HIGHEST)
        cs_t = cs.T
        cs_last = cs[l - 1:l, :]
        small = jnp.concatenate([dt, jnp.exp(cs_last - cs), jnp.exp(cs)], axis=0).astype(BF16)
        wide = jnp.dot(small, exp_ref[...], preferred_element_type=F32)
        dt_e = wide[0:l]
        dstate_e = wide[l:2 * l]
        ecs_e = wide[2 * l:3 * l]
        cdec_e = ecs_e[l - 1:l, :]

        xs = xh * dt_e
        xs_b = xs.astype(BF16)
        xd_b = (xs * dstate_e).astype(BF16)

        y_parts = []
        off_parts = []
        for g in range(SSD_GROUPS):
            bg = bm[:, g * SSD_STATE:(g + 1) * SSD_STATE]
            cg = cm[:, g * SSD_STATE:(g + 1) * SSD_STATE]
            cb = lax.dot_general(cg, bg, (((1,), (1,)), ((), ())), preferred_element_type=F32)
            for hp in range(SSD_HEADS // SSD_GROUPS // 2):
                pair = g * (SSD_HEADS // SSD_GROUPS // 2) + hp
                xs_pair = xs_b[:, pair * LANE:(pair + 1) * LANE]
                ys = []
                for h in (2 * pair, 2 * pair + 1):
                    diff = cs[:, h:h + 1] - cs_t[h:h + 1, :]
                    decay = jnp.exp(jnp.where(causal, diff, -jnp.inf))
                    ys.append(jnp.dot((cb * decay).astype(BF16), xs_pair, preferred_element_type=F32))
                y_parts.append(jnp.where(low_half, ys[0], ys[1]))
            st_g = st_ref[:, g * half:(g + 1) * half]
            off_parts.append(jnp.dot(cg, st_g.astype(BF16), preferred_element_type=F32))
            new = lax.dot_general(bg, xd_b[:, g * half:(g + 1) * half], (((0,), (0,)), ((), ())),
                                  preferred_element_type=F32)
            st_ref[:, g * half:(g + 1) * half] = st_g * cdec_e[:, g * half:(g + 1) * half] + new

        y = (jnp.concatenate(y_parts, axis=1) + jnp.concatenate(off_parts, axis=1) * ecs_e
             + xh * dskip_ref[...])
        yg = y * (z * _sigmoid(z))
        outs = []
        for g in range(SSD_GROUPS):
            yh = yg[:, g * half:(g + 1) * half]
            ms = jnp.mean(yh * yh, axis=-1, keepdims=True)
            outs.append(yh * lax.rsqrt(ms + EPS))
        o_ref[r0:r0 + l, :] = (jnp.concatenate(outs, axis=1) * ng_ref[...]).astype(o_ref.dtype)


def ssd_branch(zx, small, conv_w, conv_b, dt_bias, a_log, d_skip, norm_g, batch, seq, chunks_per_step):
    rows = chunks_per_step * SSD_CHUNK
    steps = seq // rows
    pad_h = LANE - SSD_HEADS
    dtb = jnp.pad(dt_bias, (0, pad_h)).reshape(1, LANE)
    alog = jnp.pad(a_log, (0, pad_h)).reshape(1, LANE)
    dskip = jnp.repeat(d_skip, SSD_HEAD_DIM).reshape(1, SSD_INNER)
    expand = (jnp.arange(LANE)[:, None] == jnp.arange(SSD_INNER)[None, :] // SSD_HEAD_DIM).astype(BF16)
    const = lambda b, c: (0, 0)
    return pl.pallas_call(
        functools.partial(_ssd_kernel, chunks_per_step=chunks_per_step),
        grid=(batch, steps),
        in_specs=[
            pl.BlockSpec((rows, SSD_INNER + SSD_CONV_CH), lambda b, c: (b * steps + c, 0)),
            pl.BlockSpec((rows, LANE), lambda b, c: (b * steps + c, SMALL_DT_TILE)),
            pl.BlockSpec((SSD_CONV, SSD_CONV_CH), const),
            pl.BlockSpec((1, SSD_CONV_CH), const),
            pl.BlockSpec((1, LANE), const),
            pl.BlockSpec((1, LANE), const),
            pl.BlockSpec((1, SSD_INNER), const),
            pl.BlockSpec((1, SSD_INNER), const),
            pl.BlockSpec((LANE, SSD_INNER), const),
        ],
        out_specs=pl.BlockSpec((rows, SSD_INNER), lambda b, c: (b * steps + c, 0)),
        out_shape=jax.ShapeDtypeStruct((batch * seq, SSD_INNER), BF16),
        scratch_shapes=[
            pltpu.VMEM((SSD_CHUNK + SUBLANE, SSD_CONV_CH), F32),
            pltpu.VMEM((SSD_STATE, SSD_INNER), F32),
        ],
        compiler_params=_params("parallel", "arbitrary"),
        name="ssd_branch",
    )(zx, small, conv_w, conv_b.reshape(1, -1), dtb, alog, dskip, norm_g.reshape(1, -1), expand)


def _sgu_kernel(uv_ref, g_ref, b_ref, w_ref, bias_ref, o_ref, *, chunks_per_step):
    l = SGU_CHUNK
    uv = uv_ref[...].astype(F32)
    act = 0.5 * uv * (1.0 + lax.erf(uv * (1.0 / math.sqrt(2.0))))
    u = act[:, 0:SGU_WIDTH]
    v = act[:, SGU_WIDTH:]
    mu = jnp.mean(v, axis=-1, keepdims=True)
    vc = v - mu
    var = jnp.mean(vc * vc, axis=-1, keepdims=True)
    vn = (vc * lax.rsqrt(var + LN_EPS) * g_ref[...] + b_ref[...]).astype(BF16)
    row = lax.broadcasted_iota(jnp.int32, (l, l), 0)
    col = lax.broadcasted_iota(jnp.int32, (l, l), 1)
    causal = row >= col
    for g in range(SGU_GROUPS):
        wg = jnp.where(causal, w_ref[g], 0.0).astype(BF16)
        c0 = g * LANE
        for ci in range(chunks_per_step):
            r0 = ci * l
            mixed = jnp.dot(wg, vn[r0:r0 + l, c0:c0 + LANE], preferred_element_type=F32)
            o_ref[r0:r0 + l, c0:c0 + LANE] = (
                u[r0:r0 + l, c0:c0 + LANE] * (mixed + bias_ref[:, c0:c0 + LANE])).astype(o_ref.dtype)


def sgu_branch(uv, ln_g, ln_b, w_s, b_s, chunks_per_step):
    t = uv.shape[0]
    rows = chunks_per_step * SGU_CHUNK
    bias = jnp.repeat(b_s.T, SGU_WIDTH // SGU_GROUPS, axis=1)
    const2 = lambda i: (0, 0)
    return pl.pallas_call(
        functools.partial(_sgu_kernel, chunks_per_step=chunks_per_step),
        grid=(t // rows,),
        in_specs=[
            pl.BlockSpec((rows, 2 * SGU_WIDTH), lambda i: (i, 0)),
            pl.BlockSpec((1, SGU_WIDTH), const2),
            pl.BlockSpec((1, SGU_WIDTH), const2),
            pl.BlockSpec((SGU_GROUPS, SGU_CHUNK, SGU_CHUNK), lambda i: (0, 0, 0)),
            pl.BlockSpec((SGU_CHUNK, SGU_WIDTH), const2),
        ],
        out_specs=pl.BlockSpec((rows, SGU_WIDTH), lambda i: (i, 0)),
        out_shape=jax.ShapeDtypeStruct((t, SGU_WIDTH), BF16),
        compiler_params=_params("parallel"),
        name="sgu_branch",
    )(uv, ln_g.reshape(1, -1), ln_b.reshape(1, -1), w_s, bias)


def _mla_prep_kernel(small_ref, cos_ref, sin_ref, qng_ref, kvng_ref, qhg_ref, khg_ref, wq_ref, wqp_ref, wk_ref,
                     wv_ref, q_ref, k_ref, v_ref):
    sm = small_ref[...]
    q_lat = sm[:, 0:MLA_Q_RANK]
    kv_lat = sm[:, MLA_Q_RANK:MLA_Q_RANK + MLA_KV_RANK]
    kpe = sm[:, SMALL_KPE_TILE * LANE:(SMALL_KPE_TILE + 1) * LANE]
    kpe_p = sm[:, SMALL_KPEP_TILE * LANE:(SMALL_KPEP_TILE + 1) * LANE]

    def rms(x, g):
        return x * lax.rsqrt(jnp.mean(x * x, axis=-1, keepdims=True) + EPS) * g

    ql = rms(q_lat, qng_ref[...]).astype(BF16)
    kvl = rms(kv_lat, kvng_ref[...]).astype(BF16)
    v_ref[...] = jnp.dot(kvl, wv_ref[...], preferred_element_type=F32).astype(v_ref.dtype)

    pair_w = 2 * LANE
    same_head = ((lax.broadcasted_iota(jnp.int32, (pair_w, pair_w), 0) < LANE)
                 == (lax.broadcasted_iota(jnp.int32, (pair_w, pair_w), 1) < LANE))
    ones_pair = same_head.astype(BF16)

    def head_inv_rms(x):
        sq = (x * x).astype(BF16)
        ssq = jnp.concatenate(
            [jnp.dot(sq[:, p * pair_w:(p + 1) * pair_w], ones_pair, preferred_element_type=F32)
             for p in range(MLA_HEADS // 2)], axis=1)
        return lax.rsqrt(ssq * (1.0 / MLA_QK) + EPS)

    cos = cos_ref[...]
    sin = sin_ref[...]
    rep = lambda a: jnp.tile(a, (1, MLA_HEADS))

    q_scale = MLA_QK ** -0.5 * math.log2(math.e)
    q_x = jnp.dot(ql, wq_ref[...], preferred_element_type=F32)
    q_xp = jnp.dot(ql, wqp_ref[...], preferred_element_type=F32)
    q_a = cos * qhg_ref[0:1, :] * q_scale
    q_b = sin * qhg_ref[1:2, :] * q_scale
    q_out = head_inv_rms(q_x) * (q_x * rep(q_a) + q_xp * rep(q_b))
    k_x = jnp.dot(kvl, wk_ref[...], preferred_element_type=F32) + rep(kpe)
    k_a = cos * khg_ref[0:1, :]
    k_b = sin * khg_ref[1:2, :]
    k_out = head_inv_rms(k_x) * (k_x * rep(k_a) + rep(kpe_p * k_b))
    for h in range(MLA_HEADS):
        q_ref[0, h] = q_out[:, h * LANE:(h + 1) * LANE].astype(q_ref.dtype)
        k_ref[0, h] = k_out[:, h * LANE:(h + 1) * LANE].astype(k_ref.dtype)


def _swap_rope_halves(a):
    half = MLA_ROPE // 2
    return jnp.concatenate([a[..., half:], a[..., :half]], axis=-1)


def _head_gain_rows(g):
    direct = jnp.pad(g, (0, LANE - MLA_QK))
    partner = jnp.pad(_swap_rope_halves(g[MLA_NOPE:]), (MLA_NOPE, LANE - MLA_QK))
    return jnp.stack([direct, partner])


def mla_prep(small, q_norm_g, w_uq, kv_norm_g, w_ukv, q_head_g, k_head_g, batch, seq, rows):
    steps = seq // rows
    w_uq3 = w_uq.reshape(MLA_Q_RANK, MLA_HEADS, MLA_QK)
    wq = jnp.pad(w_uq3, ((0, 0), (0, 0), (0, LANE - MLA_QK))).reshape(MLA_Q_RANK, -1).astype(BF16)
    wqp = jnp.pad(_swap_rope_halves(w_uq3[:, :, MLA_NOPE:]), ((0, 0), (0, 0), (MLA_NOPE, LANE - MLA_QK)))
    wqp = wqp.reshape(MLA_Q_RANK, -1).astype(BF16)
    w_ukv3 = w_ukv.reshape(MLA_KV_RANK, MLA_HEADS, MLA_NOPE + MLA_V)
    wk = jnp.pad(w_ukv3[:, :, :MLA_NOPE], ((0, 0), (0, 0), (0, LANE - MLA_NOPE))).reshape(MLA_KV_RANK, -1).astype(BF16)
    wv = w_ukv3[:, :, MLA_NOPE:].reshape(MLA_KV_RANK, MLA_WIDTH).astype(BF16)
    qhg = _head_gain_rows(q_head_g)
    khg = _head_gain_rows(k_head_g)
    inv = 1.0 / (ROPE_BASE ** (jnp.arange(0, MLA_ROPE, 2, dtype=F32) / MLA_ROPE))
    ang = jnp.arange(seq, dtype=F32)[:, None] * inv[None, :]
    c, s = jnp.cos(ang), jnp.sin(ang)
    ones = jnp.ones((seq, MLA_NOPE), F32)
    zeros = jnp.zeros((seq, MLA_NOPE), F32)
    tail1 = jnp.ones((seq, LANE - MLA_QK), F32)
    tail0 = jnp.zeros((seq, LANE - MLA_QK), F32)
    cos_t = jnp.concatenate([ones, c, c, tail1], axis=1)
    sin_t = jnp.concatenate([zeros, -s, s, tail0], axis=1)
    const = lambda b, i: (0, 0)
    qk_shape = jax.ShapeDtypeStruct((batch, MLA_HEADS, seq, LANE), BF16)
    return pl.pallas_call(
        _mla_prep_kernel,
        grid=(batch, steps),
        in_specs=[
            pl.BlockSpec((rows, SMALL_WIDTH), lambda b, i: (b * steps + i, 0)),
            pl.BlockSpec((rows, LANE), lambda b, i: (i, 0)),
            pl.BlockSpec((rows, LANE), lambda b, i: (i, 0)),
            pl.BlockSpec((1, MLA_Q_RANK), const),
            pl.BlockSpec((1, MLA_KV_RANK), const),
            pl.BlockSpec((2, LANE), const),
            pl.BlockSpec((2, LANE), const),
            pl.BlockSpec((MLA_Q_RANK, MLA_HEADS * LANE), const),
            pl.BlockSpec((MLA_Q_RANK, MLA_HEADS * LANE), const),
            pl.BlockSpec((MLA_KV_RANK, MLA_HEADS * LANE), const),
            pl.BlockSpec((MLA_KV_RANK, MLA_WIDTH), const),
        ],
        out_specs=[
            pl.BlockSpec((1, MLA_HEADS, rows, LANE), lambda b, i: (b, 0, i, 0)),
            pl.BlockSpec((1, MLA_HEADS, rows, LANE), lambda b, i: (b, 0, i, 0)),
            pl.BlockSpec((rows, MLA_WIDTH), lambda b, i: (b * steps + i, 0)),
        ],
        out_shape=[qk_shape, qk_shape, jax.ShapeDtypeStruct((batch * seq, MLA_WIDTH), BF16)],
        compiler_params=_params("parallel", "parallel"),
        name="mla_prep",
    )(small, cos_t, sin_t, q_norm_g.reshape(1, -1), kv_norm_g.reshape(1, -1), qhg, khg, wq, wqp, wk, wv)


FLASH_HEADS_PER_STEP = 4
FLASH_TQ = (1024, 512, 256, 128)
FLASH_TK = (512, 256, 128)


def _flash_kernel(qi_ref, kj_ref, q_ref, k_ref, v_ref, o_ref, m_ref, acc_ref, *, tq, tk):
    t = pl.program_id(2)
    qi = qi_ref[t]
    kj = kj_ref[t]
    heads = FLASH_HEADS_PER_STEP
    band = kj * tk - qi * tq

    @pl.when((kj == 0) & (band >= 0))
    def _():
        m_ref[...] = jnp.full_like(m_ref, -jnp.inf)
        acc_ref[...] = jnp.zeros_like(acc_ref)

    def step(lo, hi, triangular, first_block=False):
        rows = hi - lo
        first = lax.broadcasted_iota(jnp.int32, (tk, LANE), 1) < MLA_V
        one = jnp.ones((tk, LANE), BF16)
        if triangular:
            visible = (lax.broadcasted_iota(jnp.int32, (rows, tk), 1)
                       <= lax.broadcasted_iota(jnp.int32, (rows, tk), 0))
        for hh in range(heads):
            v = v_ref[0, :, (hh // 2) * LANE:(hh // 2 + 1) * LANE]
            v_ones = jnp.where(first, v, one) if hh % 2 == 0 else jnp.where(first, one, v)
            s = lax.dot_general(q_ref[0, hh, lo:hi, :], k_ref[0, hh], (((1,), (1,)), ((), ())),
                                preferred_element_type=F32)
            if triangular:
                s = jnp.where(visible, s, -jnp.inf)
            row_max = jnp.max(s, axis=1, keepdims=True)
            if first_block:
                m_new = jnp.broadcast_to(row_max, (rows, LANE))
                p = jnp.exp2(s - jnp.tile(m_new, (1, tk // LANE)))
                acc_ref[hh, lo:hi, :] = jnp.dot(p.astype(BF16), v_ones, preferred_element_type=F32)
            else:
                m_prev = m_ref[hh, lo:hi, :]
                m_new = jnp.maximum(m_prev, row_max)
                p = jnp.exp2(s - jnp.tile(m_new, (1, tk // LANE)))
                alpha = jnp.exp2(m_prev - m_new)
                acc_ref[hh, lo:hi, :] = (alpha * acc_ref[hh, lo:hi, :]
                                         + jnp.dot(p.astype(BF16), v_ones, preferred_element_type=F32))
            m_ref[hh, lo:hi, :] = m_new

    def finalize():
        lane = lax.broadcasted_iota(jnp.int32, (tq, LANE), 1)
        for pair in range(heads // 2):
            a0 = acc_ref[2 * pair]
            a1 = acc_ref[2 * pair + 1]
            out = jnp.where(lane < MLA_V, a0 / pltpu.roll(a0, MLA_V, 1), a1 / pltpu.roll(a1, MLA_V, 1))
            o_ref[0, :, pair * LANE:(pair + 1) * LANE] = out.astype(o_ref.dtype)

    @pl.when((band < 0) & (kj == 0))
    def _():
        step(0, tq, False, first_block=True)

    @pl.when((band < 0) & (kj > 0))
    def _():
        step(0, tq, False)

    n_band = tq // tk
    for j in range(n_band):
        @pl.when(band == j * tk)
        def _(j=j):
            step(j * tk, (j + 1) * tk, True)
            if j + 1 < n_band:
                step((j + 1) * tk, tq, False)
            else:
                finalize()


def flash_attention(q, k, v, batch, seq, tq, tk):
    assert tq % tk == 0 and seq % tq == 0
    heads = FLASH_HEADS_PER_STEP
    width = heads * MLA_V
    pairs = [(i, j) for i in range(seq // tq) for j in range((i + 1) * tq // tk)]
    qi = jnp.asarray([p[0] for p in pairs], jnp.int32)
    kj = jnp.asarray([p[1] for p in pairs], jnp.int32)
    grid_spec = pltpu.PrefetchScalarGridSpec(
        num_scalar_prefetch=2,
        grid=(batch, MLA_HEADS // heads, len(pairs)),
        in_specs=[
            pl.BlockSpec((1, heads, tq, LANE), lambda b, h, t, qi, kj: (b, h, qi[t], 0)),
            pl.BlockSpec((1, heads, tk, LANE), lambda b, h, t, qi, kj: (b, h, kj[t], 0)),
            pl.BlockSpec((1, tk, width), lambda b, h, t, qi, kj: (b, kj[t], h)),
        ],
        out_specs=pl.BlockSpec((1, tq, width), lambda b, h, t, qi, kj: (b, qi[t], h)),
        scratch_shapes=[
            pltpu.VMEM((heads, tq, LANE), F32),
            pltpu.VMEM((heads, tq, LANE), F32),
        ],
    )
    return pl.pallas_call(
        functools.partial(_flash_kernel, tq=tq, tk=tk),
        grid_spec=grid_spec,
        out_shape=jax.ShapeDtypeStruct((batch, seq, MLA_WIDTH), BF16),
        compiler_params=_params("parallel", "parallel", "arbitrary"),
        name="mla_flash_attention",
    )(qi, kj, q, k, v.reshape(batch, seq, MLA_WIDTH))


def _merge_kernel(x_ref, ys_ref, yg_ref, ym_ref, gate_ref, ws_ref, wg_ref, wm_ref, wo_ref, g2_ref,
                  x1_ref, h2_ref):
    d = D_MODEL
    gate = _sigmoid(gate_ref[...].astype(F32))
    merged = (gate[:, 0:d] * jnp.dot(ys_ref[...], ws_ref[...], preferred_element_type=F32)
              + gate[:, d:2 * d] * jnp.dot(yg_ref[...], wg_ref[...], preferred_element_type=F32)
              + gate[:, 2 * d:] * jnp.dot(ym_ref[...], wm_ref[...], preferred_element_type=F32))
    x1 = x_ref[...] + jnp.dot(merged.astype(BF16), wo_ref[...], preferred_element_type=F32)
    x1_ref[...] = x1
    ms = jnp.mean(x1 * x1, axis=-1, keepdims=True)
    h2_ref[...] = (x1 * lax.rsqrt(ms + EPS) * g2_ref[...]).astype(h2_ref.dtype)


def merge(x, y_ssd, y_sgu, y_mla, gate, w_ssd, w_sgu, w_mla, w_out, ffn_norm_g, tm, h2_dtype):
    t, d = x.shape
    row = lambda i: (i, 0)
    const = lambda i: (0, 0)
    act = pl.BlockSpec((tm, d), row)
    wspec = pl.BlockSpec((d, d), const)
    return pl.pallas_call(
        _merge_kernel,
        grid=(t // tm,),
        in_specs=[act, act, act, act, pl.BlockSpec((tm, 3 * d), row), wspec, wspec, wspec, wspec,
                  pl.BlockSpec((1, d), const)],
        out_specs=[act, act],
        out_shape=[jax.ShapeDtypeStruct((t, d), F32), jax.ShapeDtypeStruct((t, d), h2_dtype)],
        compiler_params=_params("parallel"),
        name="merge_out_proj",
    )(x, y_ssd, y_sgu, y_mla, gate, w_ssd.astype(BF16), w_sgu.astype(BF16), w_mla.astype(BF16),
      w_out.astype(BF16), ffn_norm_g.reshape(1, d))


def _ffn_kernel(h_ref, x_ref, wg_ref, wu_ref, wd_ref, o_ref, *, tf):
    h = h_ref[...]
    acc = x_ref[...]
    for c in range(wg_ref.shape[1] // tf):
        a = jnp.dot(h, wg_ref[:, c * tf:(c + 1) * tf], preferred_element_type=F32)
        b = jnp.dot(h, wu_ref[:, c * tf:(c + 1) * tf], preferred_element_type=F32)
        act = ((a * _sigmoid(a)) * b).astype(BF16)
        acc = acc + jnp.dot(act, wd_ref[c * tf:(c + 1) * tf, :], preferred_element_type=F32)
    o_ref[...] = acc


def ffn(h, x, wg, wu, wd, tm, tf):
    t, d = x.shape
    f_dim = wg.shape[1]
    row = lambda i: (i, 0)
    const = lambda i: (0, 0)
    return pl.pallas_call(
        functools.partial(_ffn_kernel, tf=tf),
        grid=(t // tm,),
        in_specs=[
            pl.BlockSpec((tm, d), row),
            pl.BlockSpec((tm, d), row),
            pl.BlockSpec((d, f_dim), const),
            pl.BlockSpec((d, f_dim), const),
            pl.BlockSpec((f_dim, d), const),
        ],
        out_specs=pl.BlockSpec((tm, d), row),
        out_shape=jax.ShapeDtypeStruct((t, d), F32),
        compiler_params=_params("parallel"),
        name="swiglu_ffn",
    )(h, x, wg.astype(BF16), wu.astype(BF16), wd.astype(BF16))


def _router_kernel(x_ref, g_ref, rw_ref, c_ref):
    x = x_ref[...]
    h = x * lax.rsqrt(jnp.mean(x * x, axis=-1, keepdims=True) + EPS) * g_ref[...]
    h_hi = h.astype(BF16)
    h_lo = (h - h_hi.astype(F32)).astype(BF16)
    w = rw_ref[...]
    w_hi = w.astype(BF16)
    w_lo = (w - w_hi.astype(F32)).astype(BF16)
    logits = (jnp.dot(h_hi, w_hi, preferred_element_type=F32) + jnp.dot(h_lo, w_hi, preferred_element_type=F32)
              + jnp.dot(h_hi, w_lo, preferred_element_type=F32))
    lane = lax.broadcasted_iota(jnp.int32, logits.shape, 1)
    lg = jnp.where(lane < N_EXPERTS, logits, -jnp.inf)
    m1 = jnp.max(lg, axis=-1, keepdims=True)
    i1 = jnp.min(jnp.where(lg == m1, lane, LANE), axis=-1, keepdims=True)
    lg2 = jnp.where(lane == i1, -jnp.inf, lg)
    m2 = jnp.max(lg2, axis=-1, keepdims=True)
    i2 = jnp.min(jnp.where(lg2 == m2, lane, LANE), axis=-1, keepdims=True)
    e2 = jnp.exp(m2 - m1)
    w1 = 1.0 / (1.0 + e2)
    w2 = e2 / (1.0 + e2)
    c_ref[...] = (jnp.where(lane == ROUTE_E1, i1.astype(F32), 0.0) + jnp.where(lane == ROUTE_E2, i2.astype(F32), 0.0)
                  + jnp.where(lane == ROUTE_W1, w1, 0.0) + jnp.where(lane == ROUTE_W2, w2, 0.0))


def router(x, g, router_w, tm):
    t, d = x.shape
    rw = jnp.pad(router_w, ((0, 0), (0, LANE - N_EXPERTS)))
    return pl.pallas_call(
        _router_kernel,
        grid=(t // tm,),
        in_specs=[pl.BlockSpec((tm, d), lambda i: (i, 0)), pl.BlockSpec((1, d), lambda i: (0, 0)),
                  pl.BlockSpec((d, LANE), lambda i: (0, 0))],
        out_specs=pl.BlockSpec((tm, LANE), lambda i: (i, 0)),
        out_shape=jax.ShapeDtypeStruct((t, LANE), F32),
        compiler_params=_params("parallel"),
        name="moe_router",
    )(x, g.reshape(1, d), rw)


def _moe_plan(route, t, tm):
    n_assign = 2 * t
    n_tiles_max = n_assign // tm + N_EXPERTS
    e = jnp.concatenate([route[:, ROUTE_E1], route[:, ROUTE_E2]]).astype(jnp.int32)
    onehot = (e[:, None] == jnp.arange(N_EXPERTS, dtype=jnp.int32)[None, :]).astype(jnp.int32)
    csum = jnp.cumsum(onehot, axis=0)
    rank = jnp.sum(csum * onehot, axis=1) - 1
    counts = csum[-1]
    tiles_per = (counts + tm - 1) // tm
    tile_end = jnp.cumsum(tiles_per)
    n_tiles = tile_end[-1]
    pos = jnp.sum(onehot * (tile_end - tiles_per)[None, :], axis=1) * tm + rank
    tile = jnp.minimum(jnp.arange(n_tiles_max, dtype=jnp.int32), n_tiles - 1)
    tile_expert = jnp.sum((tile[:, None] >= tile_end[None, :]).astype(jnp.int32), axis=1)
    shape3 = (t // tm, 1, tm)
    return (tile_expert, tile.astype(jnp.int32), n_tiles.reshape(1).astype(jnp.int32),
            pos[:t].reshape(shape3), pos[t:].reshape(shape3))


def _moe_dispatch_kernel(pos1_ref, pos2_ref, h_ref, init_hbm, xs_hbm, stage, sem, *, tm):
    del init_hbm
    i = pl.program_id(0)
    last = pl.num_programs(0) - 1
    cur = i % 2
    stage[cur] = h_ref[...]

    def row_copy(r, slot, parity):
        return pltpu.make_async_copy(stage.at[parity, pl.ds(r, 1), :], xs_hbm.at[pl.ds(slot, 1), :], sem.at[parity])

    def drain(parity):
        for _ in range(2):
            pltpu.make_async_copy(stage.at[parity], xs_hbm.at[pl.ds(0, tm), :], sem.at[parity]).wait()

    for r in range(tm):
        row_copy(r, pos1_ref[0, 0, r], cur).start()
        row_copy(r, pos2_ref[0, 0, r], cur).start()

    @pl.when(i >= 1)
    def _():
        drain(1 - cur)

    @pl.when(i == last)
    def _():
        drain(cur)


def moe_dispatch(h, pos1, pos2, n_slots, tm):
    t, d = h.shape
    smem_tile = pl.BlockSpec((1, 1, tm), lambda i: (i, 0, 0), memory_space=pltpu.SMEM)
    return pl.pallas_call(
        functools.partial(_moe_dispatch_kernel, tm=tm),
        grid=(t // tm,),
        in_specs=[smem_tile, smem_tile, pl.BlockSpec((tm, d), lambda i: (i, 0)), pl.BlockSpec(memory_space=pl.ANY)],
        out_specs=pl.BlockSpec(memory_space=pl.ANY),
        out_shape=jax.ShapeDtypeStruct((n_slots, d), h.dtype),
        scratch_shapes=[pltpu.VMEM((2, tm, d), h.dtype), pltpu.SemaphoreType.DMA((2,))],
        input_output_aliases={3: 0},
        compiler_params=_params("arbitrary"),
        name="moe_dispatch",
    )(pos1, pos2, h, jnp.zeros((n_slots, d), h.dtype))


def _moe_group_kernel(texp_ref, tile_ref, ntiles_ref, x_ref, wg_hbm, wu_hbm, wd_hbm, y_ref,
                      wg_s, wu_s, wd_s, stage_in, stage_out, sem, *, tf):
    del tile_ref
    i = pl.program_id(0)
    f_dim = wg_s.shape[1]
    n_chunk = f_dim // tf
    used = i < ntiles_ref[0]
    expert = texp_ref[i]
    new_expert = (i == 0) | (expert != texp_ref[jnp.maximum(i - 1, 0)])

    @pl.when(jnp.logical_not(used))
    def _():
        y_ref[...] = jnp.zeros_like(y_ref)

    def chunk_copy(k):
        c = k % n_chunk
        slot = k % 2
        if k < 2 * n_chunk:
            src = (wg_hbm if k < n_chunk else wu_hbm).at[expert, :, pl.ds(c * tf, tf)]
            return pltpu.make_async_copy(src, stage_in.at[slot], sem.at[slot])
        return pltpu.make_async_copy(wd_hbm.at[expert, pl.ds(c * tf, tf), :], stage_out.at[slot], sem.at[slot])

    @pl.when(used & new_expert)
    def _():
        chunk_copy(0).start()
        for k in range(3 * n_chunk):
            if k + 1 < 3 * n_chunk:
                chunk_copy(k + 1).start()
            chunk_copy(k).wait()
            c = k % n_chunk
            if k < n_chunk:
                wg_s[:, c * tf:(c + 1) * tf] = stage_in[k % 2].astype(BF16)
            elif k < 2 * n_chunk:
                wu_s[:, c * tf:(c + 1) * tf] = stage_in[k % 2].astype(BF16)
            else:
                wd_s[c * tf:(c + 1) * tf, :] = stage_out[k % 2].astype(BF16)

    @pl.when(used)
    def _():
        h = x_ref[...].astype(BF16)
        acc = jnp.zeros(y_ref.shape, F32)
        for c in range(n_chunk):
            a = jnp.dot(h, wg_s[:, c * tf:(c + 1) * tf], preferred_element_type=F32)
            b = jnp.dot(h, wu_s[:, c * tf:(c + 1) * tf], preferred_element_type=F32)
            act = ((a * _sigmoid(a)) * b).astype(BF16)
            acc = acc + jnp.dot(act, wd_s[c * tf:(c + 1) * tf, :], preferred_element_type=F32)
        y_ref[...] = acc


def moe_grouped(xs, tile_expert, tile_block, n_tiles, wg, wu, wd, tm, tf):
    n_slots, d = xs.shape
    f_dim = wg.shape[2]
    row_spec = pl.BlockSpec((tm, d), lambda i, te, tb, nt: (tb[i], 0))
    hbm_spec = pl.BlockSpec(memory_space=pl.ANY)
    grid_spec = pltpu.PrefetchScalarGridSpec(
        num_scalar_prefetch=3,
        grid=(n_slots // tm,),
        in_specs=[row_spec, hbm_spec, hbm_spec, hbm_spec],
        out_specs=pl.BlockSpec((tm, d), lambda i, te, tb, nt: (i, 0)),
        scratch_shapes=[
            pltpu.VMEM((d, f_dim), BF16), pltpu.VMEM((d, f_dim), BF16), pltpu.VMEM((f_dim, d), BF16),
            pltpu.VMEM((2, d, tf), wg.dtype), pltpu.VMEM((2, tf, d), wd.dtype),
            pltpu.SemaphoreType.DMA((2,)),
        ],
    )
    return pl.pallas_call(
        functools.partial(_moe_group_kernel, tf=tf),
        grid_spec=grid_spec,
        out_shape=jax.ShapeDtypeStruct((n_slots, d), F32),
        compiler_params=pltpu.CompilerParams(dimension_semantics=("arbitrary",),
                                             vmem_limit_bytes=MOE_VMEM_LIMIT_BYTES),
        name="moe_grouped_swiglu",
    )(tile_expert, tile_block, n_tiles, xs, wg, wu, wd)


def _moe_combine_kernel(pos1_ref, pos2_ref, pos1_next_ref, pos2_next_ref, x_ref, r_ref, y_hbm, o_ref, gbuf, sem,
                        *, tm):
    i = pl.program_id(0)
    last = pl.num_programs(0) - 1
    cur = i % 2

    def start_gather(p1_ref, p2_ref, buf):
        for r in range(tm):
            pltpu.make_async_copy(y_hbm.at[pl.ds(p1_ref[0, 0, r], 1), :], gbuf.at[buf, pl.ds(r, 1), :],
                                  sem.at[buf]).start()
            pltpu.make_async_copy(y_hbm.at[pl.ds(p2_ref[0, 0, r], 1), :], gbuf.at[buf, pl.ds(tm + r, 1), :],
                                  sem.at[buf]).start()

    @pl.when(i == 0)
    def _():
        start_gather(pos1_ref, pos2_ref, 0)

    @pl.when(i < last)
    def _():
        start_gather(pos1_next_ref, pos2_next_ref, 1 - cur)

    pltpu.make_async_copy(y_hbm.at[pl.ds(0, 2 * tm), :], gbuf.at[cur], sem.at[cur]).wait()
    r = r_ref[...]
    o_ref[...] = (x_ref[...] + r[:, ROUTE_W1:ROUTE_W1 + 1] * gbuf[cur, 0:tm, :]
                  + r[:, ROUTE_W2:ROUTE_W2 + 1] * gbuf[cur, tm:2 * tm, :])


def moe_combine(x, route, y, pos1, pos2, tm):
    t, d = x.shape
    nb = t // tm
    smem_cur = pl.BlockSpec((1, 1, tm), lambda i: (i, 0, 0), memory_space=pltpu.SMEM)
    smem_next = pl.BlockSpec((1, 1, tm), lambda i: (jnp.minimum(i + 1, nb - 1), 0, 0), memory_space=pltpu.SMEM)
    return pl.pallas_call(
        functools.partial(_moe_combine_kernel, tm=tm),
        grid=(nb,),
        in_specs=[smem_cur, smem_cur, smem_next, smem_next,
                  pl.BlockSpec((tm, d), lambda i: (i, 0)), pl.BlockSpec((tm, LANE), lambda i: (i, 0)),
                  pl.BlockSpec(memory_space=pl.ANY)],
        out_specs=pl.BlockSpec((tm, d), lambda i: (i, 0)),
        out_shape=jax.ShapeDtypeStruct((t, d), F32),
        scratch_shapes=[pltpu.VMEM((2, 2 * tm, d), F32), pltpu.SemaphoreType.DMA((2,))],
        compiler_params=_params("arbitrary"),
        name="moe_combine",
    )(pos1, pos2, pos1, pos2, x, route, y)


def _pack_in_proj(w):
    o = 0
    z = w[:, o:o + SSD_INNER]; o += SSD_INNER
    xbc = w[:, o:o + SSD_CONV_CH]; o += SSD_CONV_CH
    dt = w[:, o:o + SSD_HEADS]; o += SSD_HEADS
    uv = w[:, o:o + 2 * SGU_WIDTH]; o += 2 * SGU_WIDTH
    q_lat = w[:, o:o + MLA_Q_RANK]; o += MLA_Q_RANK
    kv_lat = w[:, o:o + MLA_KV_RANK]; o += MLA_KV_RANK
    k_pe = w[:, o:o + MLA_ROPE]; o += MLA_ROPE
    gate = w[:, o:]
    d = w.shape[0]
    zeros = lambda n: jnp.zeros((d, n), w.dtype)
    small = jnp.concatenate([
        q_lat, kv_lat,
        zeros(MLA_NOPE), k_pe, zeros(LANE - MLA_QK),
        zeros(MLA_NOPE), _swap_rope_halves(k_pe), zeros(LANE - MLA_QK),
        dt, zeros(LANE - SSD_HEADS)], axis=1)
    return (jnp.concatenate([z, xbc], axis=1).astype(BF16), uv.astype(BF16), gate.astype(BF16),
            small.astype(BF16))


def _pick(n, prefs):
    for p in prefs:
        if n % p == 0:
            return p
    return n


def kernel(x, mix_norm_g, w_in, conv_w, conv_b, dt_bias, a_log, d_skip, ssd_norm_g, sgu_ln_g, sgu_ln_b, sgu_w,
           sgu_b, q_norm_g, w_uq, kv_norm_g, w_ukv, q_head_g, k_head_g, w_br_ssd, w_br_sgu, w_br_mla, w_out,
           ffn_norm_g, ffn_w_gate, ffn_w_up, ffn_w_down, router_w, moe_w_gate, moe_w_up, moe_w_down):
    batch, seq, d = x.shape
    t = batch * seq
    depth = w_in.shape[0]
    tm = _pick(t, (1024, 512, 256, 128))
    xf = x.reshape(t, d)
    for i in range(depth):
        w_zx, w_uv, w_gate, w_small = _pack_in_proj(w_in[i])
        hn = rmsnorm(xf, mix_norm_g[i], tm)
        zx = matmul(hn, w_zx, BF16, tm, _pick(w_zx.shape[1], (1280,)), "in_proj_ssd")
        uv = matmul(hn, w_uv, BF16, tm, _pick(w_uv.shape[1], (1024,)), "in_proj_sgu")
        gate = matmul(hn, w_gate, BF16, tm, _pick(w_gate.shape[1], (1536,)), "in_proj_gate")
        small = matmul(hn, w_small, F32, tm, SMALL_WIDTH, "in_proj_small")
        y_ssd = ssd_branch(zx, small, conv_w[i], conv_b[i], dt_bias[i], a_log[i], d_skip[i], ssd_norm_g[i],
                           batch, seq, _pick(seq // SSD_CHUNK, (4, 2, 1)))
        y_sgu = sgu_branch(uv, sgu_ln_g[i], sgu_ln_b[i], sgu_w[i], sgu_b[i], _pick(seq // SGU_CHUNK, (4, 2, 1)))
        q, k, v = mla_prep(small, q_norm_g[i], w_uq[i], kv_norm_g[i], w_ukv[i], q_head_g[i], k_head_g[i],
                           batch, seq, _pick(seq, (512, 256, 128)))
        y_mla = flash_attention(q, k, v, batch, seq, _pick(seq, FLASH_TQ), _pick(seq, FLASH_TK)).reshape(t, MLA_WIDTH)
        is_moe = i % 2 == 1
        x1, h2 = merge(xf, y_ssd, y_sgu, y_mla, gate, w_br_ssd[i], w_br_sgu[i], w_br_mla[i], w_out[i],
                       ffn_norm_g[i], _pick(t, (512, 256, 128)), F32 if is_moe else BF16)
        j = i // 2
        if not is_moe:
            xf = ffn(h2, x1, ffn_w_gate[j], ffn_w_up[j], ffn_w_down[j], _pick(t, (512, 256, 128)),
                     _pick(ffn_w_gate.shape[2], (256, 128)))
        else:
            route = router(x1, ffn_norm_g[i], router_w[j], tm)
            tm_e = _pick(t, (512, 256, 128))
            tile_expert, tile_block, n_tiles, pos1, pos2 = _moe_plan(route, t, tm_e)
            xs = moe_dispatch(h2, pos1, pos2, (2 * t // tm_e + N_EXPERTS) * tm_e, tm_e)
            ys = moe_grouped(xs, tile_expert, tile_block, n_tiles, moe_w_gate[j], moe_w_up[j], moe_w_down[j],
                             tm_e, _pick(moe_w_gate.shape[3], (512, 256)))
            xf = moe_combine(x1, route, ys, pos1, pos2, tm_e)
    return xf.reshape(batch, seq, d)
```

```python
import functools
import math

import jax
import jax.numpy as jnp
from jax import lax
from jax.experimental import pallas as pl
from jax.experimental.pallas import tpu as pltpu

F32 = jnp.float32
BF16 = jnp.bfloat16

D_MODEL = 1024
SSD_HEADS = 16
SSD_HEAD_DIM = 64
SSD_INNER = SSD_HEADS * SSD_HEAD_DIM
SSD_GROUPS = 2
SSD_STATE = 128
SSD_CONV = 4
SSD_CHUNK = 128
SSD_CONV_CH = SSD_INNER + 2 * SSD_GROUPS * SSD_STATE
SGU_WIDTH = 1024
SGU_GROUPS = 8
SGU_CHUNK = 128
MLA_HEADS = 16
MLA_Q_RANK = 256
MLA_KV_RANK = 128
MLA_NOPE = 64
MLA_ROPE = 32
MLA_V = 64
MLA_QK = MLA_NOPE + MLA_ROPE
MLA_WIDTH = MLA_HEADS * MLA_V
ROPE_BASE = 10000.0
N_EXPERTS = 8
EPS = 1e-6
LN_EPS = 1e-5

LANE = 128
SUBLANE = 8
VMEM_LIMIT_BYTES = 48 * 1024 * 1024
MOE_VMEM_LIMIT_BYTES = 56 * 1024 * 1024

ROUTE_E1, ROUTE_E2, ROUTE_W1, ROUTE_W2 = 0, 1, 2, 3

SMALL_KPE_TILE = 3
SMALL_KPEP_TILE = 4
SMALL_DT_TILE = 5
SMALL_WIDTH = 6 * LANE


def _params(*semantics):
    return pltpu.CompilerParams(dimension_semantics=semantics, vmem_limit_bytes=VMEM_LIMIT_BYTES)


def _sigmoid(x):
    return 1.0 / (1.0 + jnp.exp(-x))


def _rmsnorm_kernel(x_ref, g_ref, o_ref):
    x = x_ref[...]
    ms = jnp.mean(x * x, axis=-1, keepdims=True)
    o_ref[...] = (x * lax.rsqrt(ms + EPS) * g_ref[...]).astype(o_ref.dtype)


def rmsnorm(x, g, tm):
    t, d = x.shape
    return pl.pallas_call(
        _rmsnorm_kernel,
        grid=(t // tm,),
        in_specs=[pl.BlockSpec((tm, d), lambda i: (i, 0)), pl.BlockSpec((1, d), lambda i: (0, 0))],
        out_specs=pl.BlockSpec((tm, d), lambda i: (i, 0)),
        out_shape=jax.ShapeDtypeStruct((t, d), BF16),
        compiler_params=_params("parallel"),
        name="rmsnorm",
    )(x, g.reshape(1, d))


def _matmul_kernel(a_ref, w_ref, o_ref):
    o_ref[...] = jnp.dot(a_ref[...], w_ref[...], preferred_element_type=F32).astype(o_ref.dtype)


def matmul(a, w, out_dtype, tm, tn, name):
    t, k = a.shape
    n = w.shape[1]
    return pl.pallas_call(
        _matmul_kernel,
        grid=(t // tm, n // tn),
        in_specs=[pl.BlockSpec((tm, k), lambda i, j: (i, 0)), pl.BlockSpec((k, tn), lambda i, j: (0, j))],
        out_specs=pl.BlockSpec((tm, tn), lambda i, j: (i, j)),
        out_shape=jax.ShapeDtypeStruct((t, n), out_dtype),
        compiler_params=_params("parallel", "parallel"),
        name=name,
    )(a, w)


def _ssd_kernel(zx_ref, dt_ref, convw_ref, convb_ref, dtb_ref, alog_ref, dskip_ref, ng_ref, exp_ref,
                o_ref, tail_ref, st_ref, *, chunks_per_step):
    l = SSD_CHUNK
    half = SSD_INNER // SSD_GROUPS

    @pl.when(pl.program_id(1) == 0)
    def _():
        tail_ref[...] = jnp.zeros_like(tail_ref)
        st_ref[...] = jnp.zeros_like(st_ref)

    row = lax.broadcasted_iota(jnp.int32, (l, l), 0)
    lane = lax.broadcasted_iota(jnp.int32, (l, l), 1)
    causal = row >= lane
    low_half = lane < SSD_HEAD_DIM
    tri = causal.astype(F32)
    a_neg = -jnp.exp(alog_ref[...])
    shifts = [(lane == row - (SSD_CONV - 1 - j)).astype(BF16) for j in range(SSD_CONV - 1)]

    for ci in range(chunks_per_step):
        r0 = ci * l
        z = zx_ref[r0:r0 + l, 0:SSD_INNER].astype(F32)
        cur = zx_ref[r0:r0 + l, SSD_INNER:SSD_INNER + SSD_CONV_CH]
        conv = convb_ref[...] + convw_ref[SSD_CONV - 1:SSD_CONV, :] * cur.astype(F32)
        head = jnp.zeros((SUBLANE, SSD_CONV_CH), F32)
        for j in range(SSD_CONV - 1):
            d = SSD_CONV - 1 - j
            conv = conv + convw_ref[j:j + 1, :] * jnp.dot(shifts[j], cur, preferred_element_type=F32)
            head = head + convw_ref[j:j + 1, :] * tail_ref[SUBLANE - d:2 * SUBLANE - d, :]
        conv = jnp.concatenate([conv[0:SUBLANE] + head, conv[SUBLANE:]], axis=0)
        tail_ref[0:SUBLANE, :] = cur[l - SUBLANE:l, :].astype(F32)
        xbc = conv * _sigmoid(conv)
        xh = xbc[:, 0:SSD_INNER]
        bm = xbc[:, SSD_INNER:SSD_INNER + SSD_GROUPS * SSD_STATE].astype(BF16)
        cm = xbc[:, SSD_INNER + SSD_GROUPS * SSD_STATE:].astype(BF16)

        dt_in = dt_ref[r0:r0 + l, :] + dtb_ref[...]
        dt = jnp.maximum(dt_in, 0.0) + jnp.log1p(jnp.exp(-jnp.abs(dt_in)))
        adt = dt * a_neg
        cs = jnp.dot(tri, adt, preferred_element_type=F32, precision=lax.Precision.HIGHEST)
        cs_t = cs.T
        cs_last = cs[l - 1:l, :]
        small = jnp.concatenate([dt, jnp.exp(cs_last - cs), jnp.exp(cs)], axis=0).astype(BF16)
        wide = jnp.dot(small, exp_ref[...], preferred_element_type=F32)
        dt_e = wide[0:l]
        dstate_e = wide[l:2 * l]
        ecs_e = wide[2 * l:3 * l]
        cdec_e = ecs_e[l - 1:l, :]

        xs = xh * dt_e
        xs_b = xs.astype(BF16)
        xd_b = (xs * dstate_e).astype(BF16)

        y_parts = []
        off_parts = []
        for g in range(SSD_GROUPS):
            bg = bm[:, g * SSD_STATE:(g + 1) * SSD_STATE]
            cg = cm[:, g * SSD_STATE:(g + 1) * SSD_STATE]
            cb = lax.dot_general(cg, bg, (((1,), (1,)), ((), ())), preferred_element_type=F32)
            for hp in range(SSD_HEADS // SSD_GROUPS // 2):
                pair = g * (SSD_HEADS // SSD_GROUPS // 2) + hp
                xs_pair = xs_b[:, pair * LANE:(pair + 1) * LANE]
                ys = []
                for h in (2 * pair, 2 * pair + 1):
                    diff = cs[:, h:h + 1] - cs_t[h:h + 1, :]
                    decay = jnp.exp(jnp.where(causal, diff, -jnp.inf))
                    ys.append(jnp.dot((cb * decay).astype(BF16), xs_pair, preferred_element_type=F32))
                y_parts.append(jnp.where(low_half, ys[0], ys[1]))
            st_g = st_ref[:, g * half:(g + 1) * half]
            off_parts.append(jnp.dot(cg, st_g.astype(BF16), preferred_element_type=F32))
            new = lax.dot_general(bg, xd_b[:, g * half:(g + 1) * half], (((0,), (0,)), ((), ())),
                                  preferred_element_type=F32)
            st_ref[:, g * half:(g + 1) * half] = st_g * cdec_e[:, g * half:(g + 1) * half] + new

        y = (jnp.concatenate(y_parts, axis=1) + jnp.concatenate(off_parts, axis=1) * ecs_e
             + xh * dskip_ref[...])
        yg = y * (z * _sigmoid(z))
        outs = []
        for g in range(SSD_GROUPS):
            yh = yg[:, g * half:(g + 1) * half]
            ms = jnp.mean(yh * yh, axis=-1, keepdims=True)
            outs.append(yh * lax.rsqrt(ms + EPS))
        o_ref[r0:r0 + l, :] = (jnp.concatenate(outs, axis=1) * ng_ref[...]).astype(o_ref.dtype)


def ssd_branch(zx, small, conv_w, conv_b, dt_bias, a_log, d_skip, norm_g, batch, seq, chunks_per_step):
    rows = chunks_per_step * SSD_CHUNK
    steps = seq // rows
    pad_h = LANE - SSD_HEADS
    dtb = jnp.pad(dt_bias, (0, pad_h)).reshape(1, LANE)
    alog = jnp.pad(a_log, (0, pad_h)).reshape(1, LANE)
    dskip = jnp.repeat(d_skip, SSD_HEAD_DIM).reshape(1, SSD_INNER)
    expand = (jnp.arange(LANE)[:, None] == jnp.arange(SSD_INNER)[None, :] // SSD_HEAD_DIM).astype(BF16)
    const = lambda b, c: (0, 0)
    return pl.pallas_call(
        functools.partial(_ssd_kernel, chunks_per_step=chunks_per_step),
        grid=(batch, steps),
        in_specs=[
            pl.BlockSpec((rows, SSD_INNER + SSD_CONV_CH), lambda b, c: (b * steps + c, 0)),
            pl.BlockSpec((rows, LANE), lambda b, c: (b * steps + c, SMALL_DT_TILE)),
            pl.BlockSpec((SSD_CONV, SSD_CONV_CH), const),
            pl.BlockSpec((1, SSD_CONV_CH), const),
            pl.BlockSpec((1, LANE), const),
            pl.BlockSpec((1, LANE), const),
            pl.BlockSpec((1, SSD_INNER), const),
            pl.BlockSpec((1, SSD_INNER), const),
            pl.BlockSpec((LANE, SSD_INNER), const),
        ],
        out_specs=pl.BlockSpec((rows, SSD_INNER), lambda b, c: (b * steps + c, 0)),
        out_shape=jax.ShapeDtypeStruct((batch * seq, SSD_INNER), BF16),
        scratch_shapes=[
            pltpu.VMEM((2 * SUBLANE, SSD_CONV_CH), F32),
            pltpu.VMEM((SSD_STATE, SSD_INNER), F32),
        ],
        compiler_params=_params("parallel", "arbitrary"),
        name="ssd_branch",
    )(zx, small, conv_w, conv_b.reshape(1, -1), dtb, alog, dskip, norm_g.reshape(1, -1), expand)


def _sgu_kernel(uv_ref, g_ref, b_ref, w_ref, bias_ref, o_ref, *, chunks_per_step):
    l = SGU_CHUNK
    uv = uv_ref[...].astype(F32)
    act = 0.5 * uv * (1.0 + lax.erf(uv * (1.0 / math.sqrt(2.0))))
    u = act[:, 0:SGU_WIDTH]
    v = act[:, SGU_WIDTH:]
    mu = jnp.mean(v, axis=-1, keepdims=True)
    vc = v - mu
    var = jnp.mean(vc * vc, axis=-1, keepdims=True)
    vn = (vc * lax.rsqrt(var + LN_EPS) * g_ref[...] + b_ref[...]).astype(BF16)
    row = lax.broadcasted_iota(jnp.int32, (l, l), 0)
    col = lax.broadcasted_iota(jnp.int32, (l, l), 1)
    causal = row >= col
    for g in range(SGU_GROUPS):
        wg = jnp.where(causal, w_ref[g], 0.0).astype(BF16)
        c0 = g * LANE
        for ci in range(chunks_per_step):
            r0 = ci * l
            mixed = jnp.dot(wg, vn[r0:r0 + l, c0:c0 + LANE], preferred_element_type=F32)
            o_ref[r0:r0 + l, c0:c0 + LANE] = (
                u[r0:r0 + l, c0:c0 + LANE] * (mixed + bias_ref[:, c0:c0 + LANE])).astype(o_ref.dtype)


def sgu_branch(uv, ln_g, ln_b, w_s, b_s, chunks_per_step):
    t = uv.shape[0]
    rows = chunks_per_step * SGU_CHUNK
    bias = jnp.repeat(b_s.T, SGU_WIDTH // SGU_GROUPS, axis=1)
    const2 = lambda i: (0, 0)
    return pl.pallas_call(
        functools.partial(_sgu_kernel, chunks_per_step=chunks_per_step),
        grid=(t // rows,),
        in_specs=[
            pl.BlockSpec((rows, 2 * SGU_WIDTH), lambda i: (i, 0)),
            pl.BlockSpec((1, SGU_WIDTH), const2),
            pl.BlockSpec((1, SGU_WIDTH), const2),
            pl.BlockSpec((SGU_GROUPS, SGU_CHUNK, SGU_CHUNK), lambda i: (0, 0, 0)),
            pl.BlockSpec((SGU_CHUNK, SGU_WIDTH), const2),
        ],
        out_specs=pl.BlockSpec((rows, SGU_WIDTH), lambda i: (i, 0)),
        out_shape=jax.ShapeDtypeStruct((t, SGU_WIDTH), BF16),
        compiler_params=_params("parallel"),
        name="sgu_branch",
    )(uv, ln_g.reshape(1, -1), ln_b.reshape(1, -1), w_s, bias)


def _mla_prep_kernel(small_ref, cos_ref, sin_ref, qng_ref, kvng_ref, qhg_ref, khg_ref, wq_ref, wqp_ref, wk_ref,
                     wv_ref, q_ref, k_ref, v_ref):
    sm = small_ref[...]
    q_lat = sm[:, 0:MLA_Q_RANK]
    kv_lat = sm[:, MLA_Q_RANK:MLA_Q_RANK + MLA_KV_RANK]
    kpe = sm[:, SMALL_KPE_TILE * LANE:(SMALL_KPE_TILE + 1) * LANE]
    kpe_p = sm[:, SMALL_KPEP_TILE * LANE:(SMALL_KPEP_TILE + 1) * LANE]

    def rms(x, g):
        return x * lax.rsqrt(jnp.mean(x * x, axis=-1, keepdims=True) + EPS) * g

    ql = rms(q_lat, qng_ref[...]).astype(BF16)
    kvl = rms(kv_lat, kvng_ref[...]).astype(BF16)
    v_ref[...] = jnp.dot(kvl, wv_ref[...], preferred_element_type=F32).astype(v_ref.dtype)

    pair_w = 2 * LANE
    same_head = ((lax.broadcasted_iota(jnp.int32, (pair_w, pair_w), 0) < LANE)
                 == (lax.broadcasted_iota(jnp.int32, (pair_w, pair_w), 1) < LANE))
    ones_pair = same_head.astype(BF16)

    def head_inv_rms(x):
        sq = (x * x).astype(BF16)
        ssq = jnp.concatenate(
            [jnp.dot(sq[:, p * pair_w:(p + 1) * pair_w], ones_pair, preferred_element_type=F32)
             for p in range(MLA_HEADS // 2)], axis=1)
        return lax.rsqrt(ssq * (1.0 / MLA_QK) + EPS)

    cos = cos_ref[...]
    sin = sin_ref[...]
    rep = lambda a: jnp.tile(a, (1, MLA_HEADS))

    q_scale = MLA_QK ** -0.5 * math.log2(math.e)
    q_x = jnp.dot(ql, wq_ref[...], preferred_element_type=F32)
    q_xp = jnp.dot(ql, wqp_ref[...], preferred_element_type=F32)
    q_a = cos * qhg_ref[0:1, :] * q_scale
    q_b = sin * qhg_ref[1:2, :] * q_scale
    q_out = head_inv_rms(q_x) * (q_x * rep(q_a) + q_xp * rep(q_b))
    k_x = jnp.dot(kvl, wk_ref[...], preferred_element_type=F32) + rep(kpe)
    k_a = cos * khg_ref[0:1, :]
    k_b = sin * khg_ref[1:2, :]
    k_out = head_inv_rms(k_x) * (k_x * rep(k_a) + rep(kpe_p * k_b))
    for h in range(MLA_HEADS):
        q_ref[0, h] = q_out[:, h * LANE:(h + 1) * LANE].astype(q_ref.dtype)
        k_ref[0, h] = k_out[:, h * LANE:(h + 1) * LANE].astype(k_ref.dtype)


def _swap_rope_halves(a):
    half = MLA_ROPE // 2
    return jnp.concatenate([a[..., half:], a[..., :half]], axis=-1)


def _head_gain_rows(g):
    direct = jnp.pad(g, (0, LANE - MLA_QK))
    partner = jnp.pad(_swap_rope_halves(g[MLA_NOPE:]), (MLA_NOPE, LANE - MLA_QK))
    return jnp.stack([direct, partner])


def mla_prep(small, q_norm_g, w_uq, kv_norm_g, w_ukv, q_head_g, k_head_g, batch, seq, rows):
    steps = seq // rows
    w_uq3 = w_uq.reshape(MLA_Q_RANK, MLA_HEADS, MLA_QK)
    wq = jnp.pad(w_uq3, ((0, 0), (0, 0), (0, LANE - MLA_QK))).reshape(MLA_Q_RANK, -1).astype(BF16)
    wqp = jnp.pad(_swap_rope_halves(w_uq3[:, :, MLA_NOPE:]), ((0, 0), (0, 0), (MLA_NOPE, LANE - MLA_QK)))
    wqp = wqp.reshape(MLA_Q_RANK, -1).astype(BF16)
    w_ukv3 = w_ukv.reshape(MLA_KV_RANK, MLA_HEADS, MLA_NOPE + MLA_V)
    wk = jnp.pad(w_ukv3[:, :, :MLA_NOPE], ((0, 0), (0, 0), (0, LANE - MLA_NOPE))).reshape(MLA_KV_RANK, -1).astype(BF16)
    wv = w_ukv3[:, :, MLA_NOPE:].reshape(MLA_KV_RANK, MLA_WIDTH).astype(BF16)
    qhg = _head_gain_rows(q_head_g)
    khg = _head_gain_rows(k_head_g)
    inv = 1.0 / (ROPE_BASE ** (jnp.arange(0, MLA_ROPE, 2, dtype=F32) / MLA_ROPE))
    ang = jnp.arange(seq, dtype=F32)[:, None] * inv[None, :]
    c, s = jnp.cos(ang), jnp.sin(ang)
    ones = jnp.ones((seq, MLA_NOPE), F32)
    zeros = jnp.zeros((seq, MLA_NOPE), F32)
    tail1 = jnp.ones((seq, LANE - MLA_QK), F32)
    tail0 = jnp.zeros((seq, LANE - MLA_QK), F32)
    cos_t = jnp.concatenate([ones, c, c, tail1], axis=1)
    sin_t = jnp.concatenate([zeros, -s, s, tail0], axis=1)
    const = lambda b, i: (0, 0)
    qk_shape = jax.ShapeDtypeStruct((batch, MLA_HEADS, seq, LANE), BF16)
    return pl.pallas_call(
        _mla_prep_kernel,
        grid=(batch, steps),
        in_specs=[
            pl.BlockSpec((rows, SMALL_WIDTH), lambda b, i: (b * steps + i, 0)),
            pl.BlockSpec((rows, LANE), lambda b, i: (i, 0)),
            pl.BlockSpec((rows, LANE), lambda b, i: (i, 0)),
            pl.BlockSpec((1, MLA_Q_RANK), const),
            pl.BlockSpec((1, MLA_KV_RANK), const),
            pl.BlockSpec((2, LANE), const),
            pl.BlockSpec((2, LANE), const),
            pl.BlockSpec((MLA_Q_RANK, MLA_HEADS * LANE), const),
            pl.BlockSpec((MLA_Q_RANK, MLA_HEADS * LANE), const),
            pl.BlockSpec((MLA_KV_RANK, MLA_HEADS * LANE), const),
            pl.BlockSpec((MLA_KV_RANK, MLA_WIDTH), const),
        ],
        out_specs=[
            pl.BlockSpec((1, MLA_HEADS, rows, LANE), lambda b, i: (b, 0, i, 0)),
            pl.BlockSpec((1, MLA_HEADS, rows, LANE), lambda b, i: (b, 0, i, 0)),
            pl.BlockSpec((rows, MLA_WIDTH), lambda b, i: (b * steps + i, 0)),
        ],
        out_shape=[qk_shape, qk_shape, jax.ShapeDtypeStruct((batch * seq, MLA_WIDTH), BF16)],
        compiler_params=_params("parallel", "parallel"),
        name="mla_prep",
    )(small, cos_t, sin_t, q_norm_g.reshape(1, -1), kv_norm_g.reshape(1, -1), qhg, khg, wq, wqp, wk, wv)


FLASH_HEADS_PER_STEP = 4
FLASH_TQ = (1024, 512, 256, 128)
FLASH_TK = (512, 256, 128)


def _flash_kernel(qi_ref, kj_ref, q_ref, k_ref, v_ref, o_ref, m_ref, acc_ref, *, tq, tk):
    t = pl.program_id(2)
    qi = qi_ref[t]
    kj = kj_ref[t]
    heads = FLASH_HEADS_PER_STEP
    band = kj * tk - qi * tq

    @pl.when((kj == 0) & (band >= 0))
    def _():
        m_ref[...] = jnp.full_like(m_ref, -jnp.inf)
        acc_ref[...] = jnp.zeros_like(acc_ref)

    def step(lo, hi, triangular, first_block=False):
        rows = hi - lo
        first = lax.broadcasted_iota(jnp.int32, (tk, LANE), 1) < MLA_V
        one = jnp.ones((tk, LANE), BF16)
        if triangular:
            visible = (lax.broadcasted_iota(jnp.int32, (rows, tk), 1)
                       <= lax.broadcasted_iota(jnp.int32, (rows, tk), 0))
        for hh in range(heads):
            v = v_ref[0, :, (hh // 2) * LANE:(hh // 2 + 1) * LANE]
            v_ones = jnp.where(first, v, one) if hh % 2 == 0 else jnp.where(first, one, v)
            s = lax.dot_general(q_ref[0, hh, lo:hi, :], k_ref[0, hh], (((1,), (1,)), ((), ())),
                                preferred_element_type=F32)
            if triangular:
                s = jnp.where(visible, s, -jnp.inf)
            row_max = jnp.max(s, axis=1, keepdims=True)
            if first_block:
                m_new = jnp.broadcast_to(row_max, (rows, LANE))
                p = jnp.exp2(s - jnp.tile(m_new, (1, tk // LANE)))
                acc_ref[hh, lo:hi, :] = jnp.dot(p.astype(BF16), v_ones, preferred_element_type=F32)
            else:
                m_prev = m_ref[hh, lo:hi, :]
                m_new = jnp.maximum(m_prev, row_max)
                p = jnp.exp2(s - jnp.tile(m_new, (1, tk // LANE)))
                alpha = jnp.exp2(m_prev - m_new)
                acc_ref[hh, lo:hi, :] = (alpha * acc_ref[hh, lo:hi, :]
                                         + jnp.dot(p.astype(BF16), v_ones, preferred_element_type=F32))
            m_ref[hh, lo:hi, :] = m_new

    def finalize():
        lane = lax.broadcasted_iota(jnp.int32, (tq, LANE), 1)
        for pair in range(heads // 2):
            a0 = acc_ref[2 * pair]
            a1 = acc_ref[2 * pair + 1]
            out = jnp.where(lane < MLA_V, a0 / pltpu.roll(a0, MLA_V, 1), a1 / pltpu.roll(a1, MLA_V, 1))
            o_ref[0, :, pair * LANE:(pair + 1) * LANE] = out.astype(o_ref.dtype)

    @pl.when((band < 0) & (kj == 0))
    def _():
        step(0, tq, False, first_block=True)

    @pl.when((band < 0) & (kj > 0))
    def _():
        step(0, tq, False)

    n_band = tq // tk
    for j in range(n_band):
        @pl.when(band == j * tk)
        def _(j=j):
            step(j * tk, (j + 1) * tk, True)
            if j + 1 < n_band:
                step((j + 1) * tk, tq, False)
            else:
                finalize()


def flash_attention(q, k, v, batch, seq, tq, tk):
    assert tq % tk == 0 and seq % tq == 0
    heads = FLASH_HEADS_PER_STEP
    width = heads * MLA_V
    pairs = [(i, j) for i in range(seq // tq) for j in range((i + 1) * tq // tk)]
    qi = jnp.asarray([p[0] for p in pairs], jnp.int32)
    kj = jnp.asarray([p[1] for p in pairs], jnp.int32)
    grid_spec = pltpu.PrefetchScalarGridSpec(
        num_scalar_prefetch=2,
        grid=(batch, MLA_HEADS // heads, len(pairs)),
        in_specs=[
            pl.BlockSpec((1, heads, tq, LANE), lambda b, h, t, qi, kj: (b, h, qi[t], 0)),
            pl.BlockSpec((1, heads, tk, LANE), lambda b, h, t, qi, kj: (b, h, kj[t], 0)),
            pl.BlockSpec((1, tk, width), lambda b, h, t, qi, kj: (b, kj[t], h)),
        ],
        out_specs=pl.BlockSpec((1, tq, width), lambda b, h, t, qi, kj: (b, qi[t], h)),
        scratch_shapes=[
            pltpu.VMEM((heads, tq, LANE), F32),
            pltpu.VMEM((heads, tq, LANE), F32),
        ],
    )
    return pl.pallas_call(
        functools.partial(_flash_kernel, tq=tq, tk=tk),
        grid_spec=grid_spec,
        out_shape=jax.ShapeDtypeStruct((batch, seq, MLA_WIDTH), BF16),
        compiler_params=_params("parallel", "parallel", "arbitrary"),
        name="mla_flash_attention",
    )(qi, kj, q, k, v.reshape(batch, seq, MLA_WIDTH))


def _merge_kernel(x_ref, ys_ref, yg_ref, ym_ref, gate_ref, ws_ref, wg_ref, wm_ref, wo_ref, g2_ref,
                  x1_ref, h2_ref):
    d = D_MODEL
    gate = _sigmoid(gate_ref[...].astype(F32))
    merged = (gate[:, 0:d] * jnp.dot(ys_ref[...], ws_ref[...], preferred_element_type=F32)
              + gate[:, d:2 * d] * jnp.dot(yg_ref[...], wg_ref[...], preferred_element_type=F32)
              + gate[:, 2 * d:] * jnp.dot(ym_ref[...], wm_ref[...], preferred_element_type=F32))
    x1 = x_ref[...] + jnp.dot(merged.astype(BF16), wo_ref[...], preferred_element_type=F32)
    x1_ref[...] = x1
    ms = jnp.mean(x1 * x1, axis=-1, keepdims=True)
    h2_ref[...] = (x1 * lax.rsqrt(ms + EPS) * g2_ref[...]).astype(h2_ref.dtype)


def merge(x, y_ssd, y_sgu, y_mla, gate, w_ssd, w_sgu, w_mla, w_out, ffn_norm_g, tm, h2_dtype):
    t, d = x.shape
    row = lambda i: (i, 0)
    const = lambda i: (0, 0)
    act = pl.BlockSpec((tm, d), row)
    wspec = pl.BlockSpec((d, d), const)
    return pl.pallas_call(
        _merge_kernel,
        grid=(t // tm,),
        in_specs=[act, act, act, act, pl.BlockSpec((tm, 3 * d), row), wspec, wspec, wspec, wspec,
                  pl.BlockSpec((1, d), const)],
        out_specs=[act, act],
        out_shape=[jax.ShapeDtypeStruct((t, d), F32), jax.ShapeDtypeStruct((t, d), h2_dtype)],
        compiler_params=_params("parallel"),
        name="merge_out_proj",
    )(x, y_ssd, y_sgu, y_mla, gate, w_ssd.astype(BF16), w_sgu.astype(BF16), w_mla.astype(BF16),
      w_out.astype(BF16), ffn_norm_g.reshape(1, d))


def _ffn_kernel(h_ref, x_ref, wg_ref, wu_ref, wd_ref, o_ref, *, tf):
    h = h_ref[...]
    acc = x_ref[...]
    for c in range(wg_ref.shape[1] // tf):
        a = jnp.dot(h, wg_ref[:, c * tf:(c + 1) * tf], preferred_element_type=F32)
        b = jnp.dot(h, wu_ref[:, c * tf:(c + 1) * tf], preferred_element_type=F32)
        act = ((a * _sigmoid(a)) * b).astype(BF16)
        acc = acc + jnp.dot(act, wd_ref[c * tf:(c + 1) * tf, :], preferred_element_type=F32)
    o_ref[...] = acc


def ffn(h, x, wg, wu, wd, tm, tf):
    t, d = x.shape
    f_dim = wg.shape[1]
    row = lambda i: (i, 0)
    const = lambda i: (0, 0)
    return pl.pallas_call(
        functools.partial(_ffn_kernel, tf=tf),
        grid=(t // tm,),
        in_specs=[
            pl.BlockSpec((tm, d), row),
            pl.BlockSpec((tm, d), row),
            pl.BlockSpec((d, f_dim), const),
            pl.BlockSpec((d, f_dim), const),
            pl.BlockSpec((f_dim, d), const),
        ],
        out_specs=pl.BlockSpec((tm, d), row),
        out_shape=jax.ShapeDtypeStruct((t, d), F32),
        compiler_params=_params("parallel"),
        name="swiglu_ffn",
    )(h, x, wg.astype(BF16), wu.astype(BF16), wd.astype(BF16))


def _router_kernel(x_ref, g_ref, rw_ref, c_ref):
    x = x_ref[...]
    h = x * lax.rsqrt(jnp.mean(x * x, axis=-1, keepdims=True) + EPS) * g_ref[...]
    h_hi = h.astype(BF16)
    h_lo = (h - h_hi.astype(F32)).astype(BF16)
    w = rw_ref[...]
    w_hi = w.astype(BF16)
    w_lo = (w - w_hi.astype(F32)).astype(BF16)
    logits = (jnp.dot(h_hi, w_hi, preferred_element_type=F32) + jnp.dot(h_lo, w_hi, preferred_element_type=F32)
              + jnp.dot(h_hi, w_lo, preferred_element_type=F32))
    lane = lax.broadcasted_iota(jnp.int32, logits.shape, 1)
    lg = jnp.where(lane < N_EXPERTS, logits, -jnp.inf)
    m1 = jnp.max(lg, axis=-1, keepdims=True)
    i1 = jnp.min(jnp.where(lg == m1, lane, LANE), axis=-1, keepdims=True)
    lg2 = jnp.where(lane == i1, -jnp.inf, lg)
    m2 = jnp.max(lg2, axis=-1, keepdims=True)
    i2 = jnp.min(jnp.where(lg2 == m2, lane, LANE), axis=-1, keepdims=True)
    e2 = jnp.exp(m2 - m1)
    w1 = 1.0 / (1.0 + e2)
    w2 = e2 / (1.0 + e2)
    c_ref[...] = (jnp.where(lane == ROUTE_E1, i1.astype(F32), 0.0) + jnp.where(lane == ROUTE_E2, i2.astype(F32), 0.0)
                  + jnp.where(lane == ROUTE_W1, w1, 0.0) + jnp.where(lane == ROUTE_W2, w2, 0.0))


def router(x, g, router_w, tm):
    t, d = x.shape
    rw = jnp.pad(router_w, ((0, 0), (0, LANE - N_EXPERTS)))
    return pl.pallas_call(
        _router_kernel,
        grid=(t // tm,),
        in_specs=[pl.BlockSpec((tm, d), lambda i: (i, 0)), pl.BlockSpec((1, d), lambda i: (0, 0)),
                  pl.BlockSpec((d, LANE), lambda i: (0, 0))],
        out_specs=pl.BlockSpec((tm, LANE), lambda i: (i, 0)),
        out_shape=jax.ShapeDtypeStruct((t, LANE), F32),
        compiler_params=_params("parallel"),
        name="moe_router",
    )(x, g.reshape(1, d), rw)


def _moe_plan(route, t, tm):
    n_assign = 2 * t
    n_tiles_max = n_assign // tm + N_EXPERTS
    e = jnp.concatenate([route[:, ROUTE_E1], route[:, ROUTE_E2]]).astype(jnp.int32)
    onehot = (e[:, None] == jnp.arange(N_EXPERTS, dtype=jnp.int32)[None, :]).astype(jnp.int32)
    csum = jnp.cumsum(onehot, axis=0)
    rank = jnp.sum(csum * onehot, axis=1) - 1
    counts = csum[-1]
    tiles_per = (counts + tm - 1) // tm
    tile_end = jnp.cumsum(tiles_per)
    n_tiles = tile_end[-1]
    pos = jnp.sum(onehot * (tile_end - tiles_per)[None, :], axis=1) * tm + rank
    tile = jnp.minimum(jnp.arange(n_tiles_max, dtype=jnp.int32), n_tiles - 1)
    tile_expert = jnp.sum((tile[:, None] >= tile_end[None, :]).astype(jnp.int32), axis=1)
    shape3 = (t // tm, 1, tm)
    return (tile_expert, tile.astype(jnp.int32), n_tiles.reshape(1).astype(jnp.int32),
            pos[:t].reshape(shape3), pos[t:].reshape(shape3))


def _moe_dispatch_kernel(pos1_ref, pos2_ref, h_ref, init_hbm, xs_hbm, stage, sem, *, tm):
    del init_hbm
    i = pl.program_id(0)
    last = pl.num_programs(0) - 1

    def row_copy(r, slot, parity):
        return pltpu.make_async_copy(stage.at[parity, pl.ds(r, 1), :], xs_hbm.at[pl.ds(slot, 1), :], sem.at[parity])

    def drain(parity):
        for _ in range(2):
            pltpu.make_async_copy(stage.at[parity], xs_hbm.at[pl.ds(0, tm), :], sem.at[parity]).wait()

    for parity in range(2):
        @pl.when(i % 2 == parity)
        def _(parity=parity):
            stage[parity] = h_ref[...]
            for r in range(tm):
                row_copy(r, pos1_ref[0, 0, r], parity).start(priority=0)
                row_copy(r, pos2_ref[0, 0, r], parity).start(priority=1)

            @pl.when(i >= 1)
            def _():
                drain(1 - parity)

            @pl.when(i == last)
            def _():
                drain(parity)


def moe_dispatch(h, pos1, pos2, n_slots, tm):
    t, d = h.shape
    smem_tile = pl.BlockSpec((1, 1, tm), lambda i: (i, 0, 0), memory_space=pltpu.SMEM)
    return pl.pallas_call(
        functools.partial(_moe_dispatch_kernel, tm=tm),
        grid=(t // tm,),
        in_specs=[smem_tile, smem_tile, pl.BlockSpec((tm, d), lambda i: (i, 0)), pl.BlockSpec(memory_space=pl.ANY)],
        out_specs=pl.BlockSpec(memory_space=pl.ANY),
        out_shape=jax.ShapeDtypeStruct((n_slots, d), h.dtype),
        scratch_shapes=[pltpu.VMEM((2, tm, d), h.dtype), pltpu.SemaphoreType.DMA((2,))],
        input_output_aliases={3: 0},
        compiler_params=_params("arbitrary"),
        name="moe_dispatch",
    )(pos1, pos2, h, jnp.zeros((n_slots, d), h.dtype))


def _moe_group_kernel(texp_ref, tile_ref, ntiles_ref, x_ref, wg_hbm, wu_hbm, wd_hbm, y_ref,
                      wg_s, wu_s, wd_s, stage_in, stage_out, sem, *, tf):
    del tile_ref
    i = pl.program_id(0)
    f_dim = wg_s.shape[1]
    n_chunk = f_dim // tf
    used = i < ntiles_ref[0]
    expert = texp_ref[i]
    new_expert = (i == 0) | (expert != texp_ref[jnp.maximum(i - 1, 0)])

    @pl.when(jnp.logical_not(used))
    def _():
        y_ref[...] = jnp.zeros_like(y_ref)

    def chunk_copy(k):
        c = k % n_chunk
        slot = k % 2
        if k < 2 * n_chunk:
            src = (wg_hbm if k < n_chunk else wu_hbm).at[expert, :, pl.ds(c * tf, tf)]
            return pltpu.make_async_copy(src, stage_in.at[slot], sem.at[slot])
        return pltpu.make_async_copy(wd_hbm.at[expert, pl.ds(c * tf, tf), :], stage_out.at[slot], sem.at[slot])

    @pl.when(used & new_expert)
    def _():
        chunk_copy(0).start()
        for k in range(3 * n_chunk):
            if k + 1 < 3 * n_chunk:
                chunk_copy(k + 1).start()
            chunk_copy(k).wait()
            c = k % n_chunk
            if k < n_chunk:
                wg_s[:, c * tf:(c + 1) * tf] = stage_in[k % 2].astype(BF16)
            elif k < 2 * n_chunk:
                wu_s[:, c * tf:(c + 1) * tf] = stage_in[k % 2].astype(BF16)
            else:
                wd_s[c * tf:(c + 1) * tf, :] = stage_out[k % 2].astype(BF16)

    @pl.when(used)
    def _():
        h = x_ref[...].astype(BF16)
        acc = jnp.zeros(y_ref.shape, F32)
        for c in range(n_chunk):
            a = jnp.dot(h, wg_s[:, c * tf:(c + 1) * tf], preferred_element_type=F32)
            b = jnp.dot(h, wu_s[:, c * tf:(c + 1) * tf], preferred_element_type=F32)
            act = ((a * _sigmoid(a)) * b).astype(BF16)
            acc = acc + jnp.dot(act, wd_s[c * tf:(c + 1) * tf, :], preferred_element_type=F32)
        y_ref[...] = acc


def moe_grouped(xs, tile_expert, tile_block, n_tiles, wg, wu, wd, tm, tf):
    n_slots, d = xs.shape
    f_dim = wg.shape[2]
    row_spec = pl.BlockSpec((tm, d), lambda i, te, tb, nt: (tb[i], 0))
    hbm_spec = pl.BlockSpec(memory_space=pl.ANY)
    grid_spec = pltpu.PrefetchScalarGridSpec(
        num_scalar_prefetch=3,
        grid=(n_slots // tm,),
        in_specs=[row_spec, hbm_spec, hbm_spec, hbm_spec],
        out_specs=pl.BlockSpec((tm, d), lambda i, te, tb, nt: (i, 0)),
        scratch_shapes=[
            pltpu.VMEM((d, f_dim), BF16), pltpu.VMEM((d, f_dim), BF16), pltpu.VMEM((f_dim, d), BF16),
            pltpu.VMEM((2, d, tf), wg.dtype), pltpu.VMEM((2, tf, d), wd.dtype),
            pltpu.SemaphoreType.DMA((2,)),
        ],
    )
    return pl.pallas_call(
        functools.partial(_moe_group_kernel, tf=tf),
        grid_spec=grid_spec,
        out_shape=jax.ShapeDtypeStruct((n_slots, d), F32),
        compiler_params=pltpu.CompilerParams(dimension_semantics=("arbitrary",),
                                             vmem_limit_bytes=MOE_VMEM_LIMIT_BYTES),
        name="moe_grouped_swiglu",
    )(tile_expert, tile_block, n_tiles, xs, wg, wu, wd)


def _moe_combine_kernel(pos1_ref, pos2_ref, pos1_next_ref, pos2_next_ref, x_ref, r_ref, y_hbm, o_ref, gbuf, sem,
                        *, tm):
    i = pl.program_id(0)
    last = pl.num_programs(0) - 1
    cur = i % 2

    def start_gather(p1_ref, p2_ref, buf):
        for r in range(tm):
            pltpu.make_async_copy(y_hbm.at[pl.ds(p1_ref[0, 0, r], 1), :], gbuf.at[buf, pl.ds(r, 1), :],
                                  sem.at[buf]).start(priority=0)
            pltpu.make_async_copy(y_hbm.at[pl.ds(p2_ref[0, 0, r], 1), :], gbuf.at[buf, pl.ds(tm + r, 1), :],
                                  sem.at[buf]).start(priority=1)

    @pl.when(i == 0)
    def _():
        start_gather(pos1_ref, pos2_ref, 0)

    for parity in range(2):
        @pl.when((i < last) & (cur == parity))
        def _(parity=parity):
            start_gather(pos1_next_ref, pos2_next_ref, 1 - parity)

    pltpu.make_async_copy(y_hbm.at[pl.ds(0, 2 * tm), :], gbuf.at[cur], sem.at[cur]).wait()
    r = r_ref[...]
    o_ref[...] = (x_ref[...] + r[:, ROUTE_W1:ROUTE_W1 + 1] * gbuf[cur, 0:tm, :]
                  + r[:, ROUTE_W2:ROUTE_W2 + 1] * gbuf[cur, tm:2 * tm, :])


def moe_combine(x, route, y, pos1, pos2, tm):
    t, d = x.shape
    nb = t // tm
    smem_cur = pl.BlockSpec((1, 1, tm), lambda i: (i, 0, 0), memory_space=pltpu.SMEM)
    smem_next = pl.BlockSpec((1, 1, tm), lambda i: (jnp.minimum(i + 1, nb - 1), 0, 0), memory_space=pltpu.SMEM)
    return pl.pallas_call(
        functools.partial(_moe_combine_kernel, tm=tm),
        grid=(nb,),
        in_specs=[smem_cur, smem_cur, smem_next, smem_next,
                  pl.BlockSpec((tm, d), lambda i: (i, 0)), pl.BlockSpec((tm, LANE), lambda i: (i, 0)),
                  pl.BlockSpec(memory_space=pl.ANY)],
        out_specs=pl.BlockSpec((tm, d), lambda i: (i, 0)),
        out_shape=jax.ShapeDtypeStruct((t, d), F32),
        scratch_shapes=[pltpu.VMEM((2, 2 * tm, d), F32), pltpu.SemaphoreType.DMA((2,))],
        compiler_params=_params("arbitrary"),
        name="moe_combine",
    )(pos1, pos2, pos1, pos2, x, route, y)


def _pack_in_proj(w):
    o = 0
    z = w[:, o:o + SSD_INNER]; o += SSD_INNER
    xbc = w[:, o:o + SSD_CONV_CH]; o += SSD_CONV_CH
    dt = w[:, o:o + SSD_HEADS]; o += SSD_HEADS
    uv = w[:, o:o + 2 * SGU_WIDTH]; o += 2 * SGU_WIDTH
    q_lat = w[:, o:o + MLA_Q_RANK]; o += MLA_Q_RANK
    kv_lat = w[:, o:o + MLA_KV_RANK]; o += MLA_KV_RANK
    k_pe = w[:, o:o + MLA_ROPE]; o += MLA_ROPE
    gate = w[:, o:]
    d = w.shape[0]
    zeros = lambda n: jnp.zeros((d, n), w.dtype)
    small = jnp.concatenate([
        q_lat, kv_lat,
        zeros(MLA_NOPE), k_pe, zeros(LANE - MLA_QK),
        zeros(MLA_NOPE), _swap_rope_halves(k_pe), zeros(LANE - MLA_QK),
        dt, zeros(LANE - SSD_HEADS)], axis=1)
    return (jnp.concatenate([z, xbc], axis=1).astype(BF16), uv.astype(BF16), gate.astype(BF16),
            small.astype(BF16))


def _pick(n, prefs):
    for p in prefs:
        if n % p == 0:
            return p
    return n


def kernel(x, mix_norm_g, w_in, conv_w, conv_b, dt_bias, a_log, d_skip, ssd_norm_g, sgu_ln_g, sgu_ln_b, sgu_w,
           sgu_b, q_norm_g, w_uq, kv_norm_g, w_ukv, q_head_g, k_head_g, w_br_ssd, w_br_sgu, w_br_mla, w_out,
           ffn_norm_g, ffn_w_gate, ffn_w_up, ffn_w_down, router_w, moe_w_gate, moe_w_up, moe_w_down):
    batch, seq, d = x.shape
    t = batch * seq
    depth = w_in.shape[0]
    tm = _pick(t, (1024, 512, 256, 128))
    xf = x.reshape(t, d)
    for i in range(depth):
        w_zx, w_uv, w_gate, w_small = _pack_in_proj(w_in[i])
        hn = rmsnorm(xf, mix_norm_g[i], tm)
        zx = matmul(hn, w_zx, BF16, tm, _pick(w_zx.shape[1], (1280,)), "in_proj_ssd")
        uv = matmul(hn, w_uv, BF16, tm, _pick(w_uv.shape[1], (1024,)), "in_proj_sgu")
        gate = matmul(hn, w_gate, BF16, tm, _pick(w_gate.shape[1], (1536,)), "in_proj_gate")
        small = matmul(hn, w_small, F32, tm, SMALL_WIDTH, "in_proj_small")
        y_ssd = ssd_branch(zx, small, conv_w[i], conv_b[i], dt_bias[i], a_log[i], d_skip[i], ssd_norm_g[i],
                           batch, seq, _pick(seq // SSD_CHUNK, (4, 2, 1)))
        y_sgu = sgu_branch(uv, sgu_ln_g[i], sgu_ln_b[i], sgu_w[i], sgu_b[i], _pick(seq // SGU_CHUNK, (4, 2, 1)))
        q, k, v = mla_prep(small, q_norm_g[i], w_uq[i], kv_norm_g[i], w_ukv[i], q_head_g[i], k_head_g[i],
                           batch, seq, _pick(seq, (512, 256, 128)))
        y_mla = flash_attention(q, k, v, batch, seq, _pick(seq, FLASH_TQ), _pick(seq, FLASH_TK)).reshape(t, MLA_WIDTH)
        is_moe = i % 2 == 1
        x1, h2 = merge(xf, y_ssd, y_sgu, y_mla, gate, w_br_ssd[i], w_br_sgu[i], w_br_mla[i], w_out[i],
                       ffn_norm_g[i], _pick(t, (512, 256, 128)), F32 if is_moe else BF16)
        j = i // 2
        if not is_moe:
            xf = ffn(h2, x1, ffn_w_gate[j], ffn_w_up[j], ffn_w_down[j], _pick(t, (512, 256, 128)),
                     _pick(ffn_w_gate.shape[2], (256, 128)))
        else:
            route = router(x1, ffn_norm_g[i], router_w[j], tm)
            tm_e = _pick(t, (512, 256, 128))
            tile_expert, tile_block, n_tiles, pos1, pos2 = _moe_plan(route, t, tm_e)
            xs = moe_dispatch(h2, pos1, pos2, (2 * t // tm_e + N_EXPERTS) * tm_e, tm_e)
            ys = moe_grouped(xs, tile_expert, tile_block, n_tiles, moe_w_gate[j], moe_w_up[j], moe_w_down[j],
                             tm_e, _pick(moe_w_gate.shape[3], (512, 256)))
            xf = moe_combine(x1, route, ys, pos1, pos2, tm_e)
    return xf.reshape(batch, seq, d)
```

```python
import functools
import math

import jax
import jax.numpy as jnp
from jax import lax
from jax.experimental import pallas as pl
from jax.experimental.pallas import tpu as pltpu

F32 = jnp.float32
BF16 = jnp.bfloat16

D_MODEL = 1024
SSD_HEADS = 16
SSD_HEAD_DIM = 64
SSD_INNER = SSD_HEADS * SSD_HEAD_DIM
SSD_GROUPS = 2
SSD_STATE = 128
SSD_CONV = 4
SSD_CHUNK = 128
SSD_CONV_CH = SSD_INNER + 2 * SSD_GROUPS * SSD_STATE
SGU_WIDTH = 1024
SGU_GROUPS = 8
SGU_CHUNK = 128
MLA_HEADS = 16
MLA_Q_RANK = 256
MLA_KV_RANK = 128
MLA_NOPE = 64
MLA_ROPE = 32
MLA_V = 64
MLA_QK = MLA_NOPE + MLA_ROPE
MLA_WIDTH = MLA_HEADS * MLA_V
ROPE_BASE = 10000.0
N_EXPERTS = 8
EPS = 1e-6
LN_EPS = 1e-5

LANE = 128
SUBLANE = 8
VMEM_LIMIT_BYTES = 48 * 1024 * 1024
MOE_VMEM_LIMIT_BYTES = 56 * 1024 * 1024

ROUTE_E1, ROUTE_E2, ROUTE_W1, ROUTE_W2 = 0, 1, 2, 3

SMALL_KPE_TILE = 3
SMALL_KPEP_TILE = 4
SMALL_DT_TILE = 5
SMALL_WIDTH = 6 * LANE


def _params(*semantics):
    return pltpu.CompilerParams(dimension_semantics=semantics, vmem_limit_bytes=VMEM_LIMIT_BYTES)


def _sigmoid(x):
    return 1.0 / (1.0 + jnp.exp(-x))


def _rmsnorm_kernel(x_ref, g_ref, o_ref):
    x = x_ref[...]
    ms = jnp.mean(x * x, axis=-1, keepdims=True)
    o_ref[...] = (x * lax.rsqrt(ms + EPS) * g_ref[...]).astype(o_ref.dtype)


def rmsnorm(x, g, tm):
    t, d = x.shape
    return pl.pallas_call(
        _rmsnorm_kernel,
        grid=(t // tm,),
        in_specs=[pl.BlockSpec((tm, d), lambda i: (i, 0)), pl.BlockSpec((1, d), lambda i: (0, 0))],
        out_specs=pl.BlockSpec((tm, d), lambda i: (i, 0)),
        out_shape=jax.ShapeDtypeStruct((t, d), BF16),
        compiler_params=_params("parallel"),
        name="rmsnorm",
    )(x, g.reshape(1, d))


def _matmul_kernel(a_ref, w_ref, o_ref):
    o_ref[...] = jnp.dot(a_ref[...], w_ref[...], preferred_element_type=F32).astype(o_ref.dtype)


def matmul(a, w, out_dtype, tm, tn, name):
    t, k = a.shape
    n = w.shape[1]
    return pl.pallas_call(
        _matmul_kernel,
        grid=(t // tm, n // tn),
        in_specs=[pl.BlockSpec((tm, k), lambda i, j: (i, 0)), pl.BlockSpec((k, tn), lambda i, j: (0, j))],
        out_specs=pl.BlockSpec((tm, tn), lambda i, j: (i, j)),
        out_shape=jax.ShapeDtypeStruct((t, n), out_dtype),
        compiler_params=_params("parallel", "parallel"),
        name=name,
    )(a, w)


def _ssd_kernel(zx_ref, dt_ref, convw_ref, convb_ref, dtb_ref, alog_ref, dskip_ref, ng_ref, exp_ref,
                o_ref, tail_ref, st_ref, *, chunks_per_step):
    l = SSD_CHUNK
    half = SSD_INNER // SSD_GROUPS

    @pl.when(pl.program_id(1) == 0)
    def _():
        tail_ref[...] = jnp.zeros_like(tail_ref)
        st_ref[...] = jnp.zeros_like(st_ref)

    row = lax.broadcasted_iota(jnp.int32, (l, l), 0)
    lane = lax.broadcasted_iota(jnp.int32, (l, l), 1)
    causal = row >= lane
    low_half = lane < SSD_HEAD_DIM
    tri = causal.astype(F32)
    a_neg = -jnp.exp(alog_ref[...])
    shifts = [(lane == row - (SSD_CONV - 1 - j)).astype(BF16) for j in range(SSD_CONV - 1)]

    for ci in range(chunks_per_step):
        r0 = ci * l
        z = zx_ref[r0:r0 + l, 0:SSD_INNER].astype(F32)
        cur = zx_ref[r0:r0 + l, SSD_INNER:SSD_INNER + SSD_CONV_CH]
        conv = convb_ref[...] + convw_ref[SSD_CONV - 1:SSD_CONV, :] * cur.astype(F32)
        head = jnp.zeros((SUBLANE, SSD_CONV_CH), F32)
        for j in range(SSD_CONV - 1):
            d = SSD_CONV - 1 - j
            conv = conv + convw_ref[j:j + 1, :] * jnp.dot(shifts[j], cur, preferred_element_type=F32)
            head = head + convw_ref[j:j + 1, :] * tail_ref[SUBLANE - d:2 * SUBLANE - d, :]
        conv = jnp.concatenate([conv[0:SUBLANE] + head, conv[SUBLANE:]], axis=0)
        tail_ref[0:SUBLANE, :] = cur[l - SUBLANE:l, :].astype(F32)
        xbc = conv * _sigmoid(conv)
        xh = xbc[:, 0:SSD_INNER]
        bm = xbc[:, SSD_INNER:SSD_INNER + SSD_GROUPS * SSD_STATE].astype(BF16)
        cm = xbc[:, SSD_INNER + SSD_GROUPS * SSD_STATE:].astype(BF16)

        dt_in = dt_ref[r0:r0 + l, :] + dtb_ref[...]
        dt = jnp.maximum(dt_in, 0.0) + jnp.log1p(jnp.exp(-jnp.abs(dt_in)))
        adt = dt * a_neg
        cs = jnp.dot(tri, adt, preferred_element_type=F32, precision=lax.Precision.HIGHEST)
        cs_t = cs.T
        cs_last = cs[l - 1:l, :]
        small = jnp.concatenate([dt, jnp.exp(cs_last - cs), jnp.exp(cs)], axis=0).astype(BF16)
        wide = jnp.dot(small, exp_ref[...], preferred_element_type=F32)
        dt_e = wide[0:l]
        dstate_e = wide[l:2 * l]
        ecs_e = wide[2 * l:3 * l]
        cdec_e = ecs_e[l - 1:l, :]

        xs = xh * dt_e
        xs_b = xs.astype(BF16)
        xd_b = (xs * dstate_e).astype(BF16)

        y_parts = []
        off_parts = []
        for g in range(SSD_GROUPS):
            bg = bm[:, g * SSD_STATE:(g + 1) * SSD_STATE]
            cg = cm[:, g * SSD_STATE:(g + 1) * SSD_STATE]
            cb = lax.dot_general(cg, bg, (((1,), (1,)), ((), ())), preferred_element_type=F32)
            for hp in range(SSD_HEADS // SSD_GROUPS // 2):
                pair = g * (SSD_HEADS // SSD_GROUPS // 2) + hp
                xs_pair = xs_b[:, pair * LANE:(pair + 1) * LANE]
                ys = []
                for h in (2 * pair, 2 * pair + 1):
                    diff = cs[:, h:h + 1] - cs_t[h:h + 1, :]
                    decay = jnp.exp(jnp.where(causal, diff, -jnp.inf))
                    ys.append(jnp.dot((cb * decay).astype(BF16), xs_pair, preferred_element_type=F32))
                y_parts.append(jnp.where(low_half, ys[0], ys[1]))
            st_g = st_ref[:, g * half:(g + 1) * half]
            off_parts.append(jnp.dot(cg, st_g.astype(BF16), preferred_element_type=F32))
            new = lax.dot_general(bg, xd_b[:, g * half:(g + 1) * half], (((0,), (0,)), ((), ())),
                                  preferred_element_type=F32)
            st_ref[:, g * half:(g + 1) * half] = st_g * cdec_e[:, g * half:(g + 1) * half] + new

        y = (jnp.concatenate(y_parts, axis=1) + jnp.concatenate(off_parts, axis=1) * ecs_e
             + xh * dskip_ref[...])
        yg = y * (z * _sigmoid(z))
        outs = []
        for g in range(SSD_GROUPS):
            yh = yg[:, g * half:(g + 1) * half]
            ms = jnp.mean(yh * yh, axis=-1, keepdims=True)
            outs.append(yh * lax.rsqrt(ms + EPS))
        o_ref[r0:r0 + l, :] = (jnp.concatenate(outs, axis=1) * ng_ref[...]).astype(o_ref.dtype)


def ssd_branch(zx, small, conv_w, conv_b, dt_bias, a_log, d_skip, norm_g, batch, seq, chunks_per_step):
    rows = chunks_per_step * SSD_CHUNK
    steps = seq // rows
    pad_h = LANE - SSD_HEADS
    dtb = jnp.pad(dt_bias, (0, pad_h)).reshape(1, LANE)
    alog = jnp.pad(a_log, (0, pad_h)).reshape(1, LANE)
    dskip = jnp.repeat(d_skip, SSD_HEAD_DIM).reshape(1, SSD_INNER)
    expand = (jnp.arange(LANE)[:, None] == jnp.arange(SSD_INNER)[None, :] // SSD_HEAD_DIM).astype(BF16)
    const = lambda b, c: (0, 0)
    return pl.pallas_call(
        functools.partial(_ssd_kernel, chunks_per_step=chunks_per_step),
        grid=(batch, steps),
        in_specs=[
            pl.BlockSpec((rows, SSD_INNER + SSD_CONV_CH), lambda b, c: (b * steps + c, 0)),
            pl.BlockSpec((rows, LANE), lambda b, c: (b * steps + c, SMALL_DT_TILE)),
            pl.BlockSpec((SSD_CONV, SSD_CONV_CH), const),
            pl.BlockSpec((1, SSD_CONV_CH), const),
            pl.BlockSpec((1, LANE), const),
            pl.BlockSpec((1, LANE), const),
            pl.BlockSpec((1, SSD_INNER), const),
            pl.BlockSpec((1, SSD_INNER), const),
            pl.BlockSpec((LANE, SSD_INNER), const),
        ],
        out_specs=pl.BlockSpec((rows, SSD_INNER), lambda b, c: (b * steps + c, 0)),
        out_shape=jax.ShapeDtypeStruct((batch * seq, SSD_INNER), BF16),
        scratch_shapes=[
            pltpu.VMEM((2 * SUBLANE, SSD_CONV_CH), F32),
            pltpu.VMEM((SSD_STATE, SSD_INNER), F32),
        ],
        compiler_params=_params("parallel", "arbitrary"),
        name="ssd_branch",
    )(zx, small, conv_w, conv_b.reshape(1, -1), dtb, alog, dskip, norm_g.reshape(1, -1), expand)


def _sgu_kernel(uv_ref, g_ref, b_ref, w_ref, bias_ref, o_ref, *, chunks_per_step):
    l = SGU_CHUNK
    uv = uv_ref[...].astype(F32)
    act = 0.5 * uv * (1.0 + lax.erf(uv * (1.0 / math.sqrt(2.0))))
    u = act[:, 0:SGU_WIDTH]
    v = act[:, SGU_WIDTH:]
    mu = jnp.mean(v, axis=-1, keepdims=True)
    vc = v - mu
    var = jnp.mean(vc * vc, axis=-1, keepdims=True)
    vn = (vc * lax.rsqrt(var + LN_EPS) * g_ref[...] + b_ref[...]).astype(BF16)
    row = lax.broadcasted_iota(jnp.int32, (l, l), 0)
    col = lax.broadcasted_iota(jnp.int32, (l, l), 1)
    causal = row >= col
    for g in range(SGU_GROUPS):
        wg = jnp.where(causal, w_ref[g], 0.0).astype(BF16)
        c0 = g * LANE
        for ci in range(chunks_per_step):
            r0 = ci * l
            mixed = jnp.dot(wg, vn[r0:r0 + l, c0:c0 + LANE], preferred_element_type=F32)
            o_ref[r0:r0 + l, c0:c0 + LANE] = (
                u[r0:r0 + l, c0:c0 + LANE] * (mixed + bias_ref[:, c0:c0 + LANE])).astype(o_ref.dtype)


def sgu_branch(uv, ln_g, ln_b, w_s, b_s, chunks_per_step):
    t = uv.shape[0]
    rows = chunks_per_step * SGU_CHUNK
    bias = jnp.repeat(b_s.T, SGU_WIDTH // SGU_GROUPS, axis=1)
    const2 = lambda i: (0, 0)
    return pl.pallas_call(
        functools.partial(_sgu_kernel, chunks_per_step=chunks_per_step),
        grid=(t // rows,),
        in_specs=[
            pl.BlockSpec((rows, 2 * SGU_WIDTH), lambda i: (i, 0)),
            pl.BlockSpec((1, SGU_WIDTH), const2),
            pl.BlockSpec((1, SGU_WIDTH), const2),
            pl.BlockSpec((SGU_GROUPS, SGU_CHUNK, SGU_CHUNK), lambda i: (0, 0, 0)),
            pl.BlockSpec((SGU_CHUNK, SGU_WIDTH), const2),
        ],
        out_specs=pl.BlockSpec((rows, SGU_WIDTH), lambda i: (i, 0)),
        out_shape=jax.ShapeDtypeStruct((t, SGU_WIDTH), BF16),
        compiler_params=_params("parallel"),
        name="sgu_branch",
    )(uv, ln_g.reshape(1, -1), ln_b.reshape(1, -1), w_s, bias)


def _mla_prep_kernel(small_ref, cos_ref, sin_ref, qng_ref, kvng_ref, qhg_ref, khg_ref, wq_ref, wqp_ref, wk_ref,
                     wv_ref, q_ref, k_ref, v_ref):
    sm = small_ref[...]
    q_lat = sm[:, 0:MLA_Q_RANK]
    kv_lat = sm[:, MLA_Q_RANK:MLA_Q_RANK + MLA_KV_RANK]
    kpe = sm[:, SMALL_KPE_TILE * LANE:(SMALL_KPE_TILE + 1) * LANE]
    kpe_p = sm[:, SMALL_KPEP_TILE * LANE:(SMALL_KPEP_TILE + 1) * LANE]

    def rms(x, g):
        return x * lax.rsqrt(jnp.mean(x * x, axis=-1, keepdims=True) + EPS) * g

    ql = rms(q_lat, qng_ref[...]).astype(BF16)
    kvl = rms(kv_lat, kvng_ref[...]).astype(BF16)
    v_ref[...] = jnp.dot(kvl, wv_ref[...], preferred_element_type=F32).astype(v_ref.dtype)

    pair_w = 2 * LANE
    same_head = ((lax.broadcasted_iota(jnp.int32, (pair_w, pair_w), 0) < LANE)
                 == (lax.broadcasted_iota(jnp.int32, (pair_w, pair_w), 1) < LANE))
    ones_pair = same_head.astype(BF16)

    def head_inv_rms(x):
        sq = (x * x).astype(BF16)
        ssq = jnp.concatenate(
            [jnp.dot(sq[:, p * pair_w:(p + 1) * pair_w], ones_pair, preferred_element_type=F32)
             for p in range(MLA_HEADS // 2)], axis=1)
        return lax.rsqrt(ssq * (1.0 / MLA_QK) + EPS)

    cos = cos_ref[...]
    sin = sin_ref[...]
    rep = lambda a: jnp.tile(a, (1, MLA_HEADS))

    q_scale = MLA_QK ** -0.5 * math.log2(math.e)
    q_x = jnp.dot(ql, wq_ref[...], preferred_element_type=F32)
    q_xp = jnp.dot(ql, wqp_ref[...], preferred_element_type=F32)
    q_a = cos * qhg_ref[0:1, :] * q_scale
    q_b = sin * qhg_ref[1:2, :] * q_scale
    q_out = head_inv_rms(q_x) * (q_x * rep(q_a) + q_xp * rep(q_b))
    k_x = jnp.dot(kvl, wk_ref[...], preferred_element_type=F32) + rep(kpe)
    k_a = cos * khg_ref[0:1, :]
    k_b = sin * khg_ref[1:2, :]
    k_out = head_inv_rms(k_x) * (k_x * rep(k_a) + rep(kpe_p * k_b))
    for h in range(MLA_HEADS):
        q_ref[0, h] = q_out[:, h * LANE:(h + 1) * LANE].astype(q_ref.dtype)
        k_ref[0, h] = k_out[:, h * LANE:(h + 1) * LANE].astype(k_ref.dtype)


def _swap_rope_halves(a):
    half = MLA_ROPE // 2
    return jnp.concatenate([a[..., half:], a[..., :half]], axis=-1)


def _head_gain_rows(g):
    direct = jnp.pad(g, (0, LANE - MLA_QK))
    partner = jnp.pad(_swap_rope_halves(g[MLA_NOPE:]), (MLA_NOPE, LANE - MLA_QK))
    return jnp.stack([direct, partner])


def mla_prep(small, q_norm_g, w_uq, kv_norm_g, w_ukv, q_head_g, k_head_g, batch, seq, rows):
    steps = seq // rows
    w_uq3 = w_uq.reshape(MLA_Q_RANK, MLA_HEADS, MLA_QK)
    wq = jnp.pad(w_uq3, ((0, 0), (0, 0), (0, LANE - MLA_QK))).reshape(MLA_Q_RANK, -1).astype(BF16)
    wqp = jnp.pad(_swap_rope_halves(w_uq3[:, :, MLA_NOPE:]), ((0, 0), (0, 0), (MLA_NOPE, LANE - MLA_QK)))
    wqp = wqp.reshape(MLA_Q_RANK, -1).astype(BF16)
    w_ukv3 = w_ukv.reshape(MLA_KV_RANK, MLA_HEADS, MLA_NOPE + MLA_V)
    wk = jnp.pad(w_ukv3[:, :, :MLA_NOPE], ((0, 0), (0, 0), (0, LANE - MLA_NOPE))).reshape(MLA_KV_RANK, -1).astype(BF16)
    wv = w_ukv3[:, :, MLA_NOPE:].reshape(MLA_KV_RANK, MLA_WIDTH).astype(BF16)
    qhg = _head_gain_rows(q_head_g)
    khg = _head_gain_rows(k_head_g)
    inv = 1.0 / (ROPE_BASE ** (jnp.arange(0, MLA_ROPE, 2, dtype=F32) / MLA_ROPE))
    ang = jnp.arange(seq, dtype=F32)[:, None] * inv[None, :]
    c, s = jnp.cos(ang), jnp.sin(ang)
    ones = jnp.ones((seq, MLA_NOPE), F32)
    zeros = jnp.zeros((seq, MLA_NOPE), F32)
    tail1 = jnp.ones((seq, LANE - MLA_QK), F32)
    tail0 = jnp.zeros((seq, LANE - MLA_QK), F32)
    cos_t = jnp.concatenate([ones, c, c, tail1], axis=1)
    sin_t = jnp.concatenate([zeros, -s, s, tail0], axis=1)
    const = lambda b, i: (0, 0)
    qk_shape = jax.ShapeDtypeStruct((batch, MLA_HEADS, seq, LANE), BF16)
    return pl.pallas_call(
        _mla_prep_kernel,
        grid=(batch, steps),
        in_specs=[
            pl.BlockSpec((rows, SMALL_WIDTH), lambda b, i: (b * steps + i, 0)),
            pl.BlockSpec((rows, LANE), lambda b, i: (i, 0)),
            pl.BlockSpec((rows, LANE), lambda b, i: (i, 0)),
            pl.BlockSpec((1, MLA_Q_RANK), const),
            pl.BlockSpec((1, MLA_KV_RANK), const),
            pl.BlockSpec((2, LANE), const),
            pl.BlockSpec((2, LANE), const),
            pl.BlockSpec((MLA_Q_RANK, MLA_HEADS * LANE), const),
            pl.BlockSpec((MLA_Q_RANK, MLA_HEADS * LANE), const),
            pl.BlockSpec((MLA_KV_RANK, MLA_HEADS * LANE), const),
            pl.BlockSpec((MLA_KV_RANK, MLA_WIDTH), const),
        ],
        out_specs=[
            pl.BlockSpec((1, MLA_HEADS, rows, LANE), lambda b, i: (b, 0, i, 0)),
            pl.BlockSpec((1, MLA_HEADS, rows, LANE), lambda b, i: (b, 0, i, 0)),
            pl.BlockSpec((rows, MLA_WIDTH), lambda b, i: (b * steps + i, 0)),
        ],
        out_shape=[qk_shape, qk_shape, jax.ShapeDtypeStruct((batch * seq, MLA_WIDTH), BF16)],
        compiler_params=_params("parallel", "parallel"),
        name="mla_prep",
    )(small, cos_t, sin_t, q_norm_g.reshape(1, -1), kv_norm_g.reshape(1, -1), qhg, khg, wq, wqp, wk, wv)


FLASH_HEADS_PER_STEP = 8
FLASH_TQ = (1024, 512, 256, 128)
FLASH_TK = (512, 256, 128)


def _flash_kernel(qi_ref, kj_ref, q_ref, k_ref, v_ref, o_ref, m_ref, acc_ref, *, tq, tk):
    t = pl.program_id(2)
    qi = qi_ref[t]
    kj = kj_ref[t]
    heads = FLASH_HEADS_PER_STEP
    band = kj * tk - qi * tq

    @pl.when((kj == 0) & (band >= 0))
    def _():
        m_ref[...] = jnp.full_like(m_ref, -jnp.inf)
        acc_ref[...] = jnp.zeros_like(acc_ref)

    def step(lo, hi, triangular, first_block=False):
        rows = hi - lo
        first = lax.broadcasted_iota(jnp.int32, (tk, LANE), 1) < MLA_V
        one = jnp.ones((tk, LANE), BF16)
        if triangular:
            visible = (lax.broadcasted_iota(jnp.int32, (rows, tk), 1)
                       <= lax.broadcasted_iota(jnp.int32, (rows, tk), 0))
        for hh in range(heads):
            v = v_ref[0, :, (hh // 2) * LANE:(hh // 2 + 1) * LANE]
            v_ones = jnp.where(first, v, one) if hh % 2 == 0 else jnp.where(first, one, v)
            s = lax.dot_general(q_ref[0, hh, lo:hi, :], k_ref[0, hh], (((1,), (1,)), ((), ())),
                                preferred_element_type=F32)
            if triangular:
                s = jnp.where(visible, s, -jnp.inf)
            row_max = jnp.max(s, axis=1, keepdims=True)
            if first_block:
                m_new = jnp.broadcast_to(row_max, (rows, LANE))
                p = jnp.exp2(s - jnp.tile(m_new, (1, tk // LANE)))
                acc_ref[hh, lo:hi, :] = jnp.dot(p.astype(BF16), v_ones, preferred_element_type=F32)
            else:
                m_prev = m_ref[hh, lo:hi, :]
                m_new = jnp.maximum(m_prev, row_max)
                p = jnp.exp2(s - jnp.tile(m_new, (1, tk // LANE)))
                alpha = jnp.exp2(m_prev - m_new)
                acc_ref[hh, lo:hi, :] = (alpha * acc_ref[hh, lo:hi, :]
                                         + jnp.dot(p.astype(BF16), v_ones, preferred_element_type=F32))
            m_ref[hh, lo:hi, :] = m_new

    def finalize():
        lane = lax.broadcasted_iota(jnp.int32, (tq, LANE), 1)
        for pair in range(heads // 2):
            a0 = acc_ref[2 * pair]
            a1 = acc_ref[2 * pair + 1]
            out = jnp.where(lane < MLA_V, a0 / pltpu.roll(a0, MLA_V, 1), a1 / pltpu.roll(a1, MLA_V, 1))
            o_ref[0, :, pair * LANE:(pair + 1) * LANE] = out.astype(o_ref.dtype)

    @pl.when((band < 0) & (kj == 0))
    def _():
        step(0, tq, False, first_block=True)

    @pl.when((band < 0) & (kj > 0))
    def _():
        step(0, tq, False)

    n_band = tq // tk
    for j in range(n_band):
        @pl.when(band == j * tk)
        def _(j=j):
            step(j * tk, (j + 1) * tk, True)
            if j + 1 < n_band:
                step((j + 1) * tk, tq, False)
            else:
                finalize()


def flash_attention(q, k, v, batch, seq, tq, tk):
    assert tq % tk == 0 and seq % tq == 0
    heads = FLASH_HEADS_PER_STEP
    width = heads * MLA_V
    pairs = [(i, j) for i in range(seq // tq) for j in range((i + 1) * tq // tk)]
    qi = jnp.asarray([p[0] for p in pairs], jnp.int32)
    kj = jnp.asarray([p[1] for p in pairs], jnp.int32)
    grid_spec = pltpu.PrefetchScalarGridSpec(
        num_scalar_prefetch=2,
        grid=(batch, MLA_HEADS // heads, len(pairs)),
        in_specs=[
            pl.BlockSpec((1, heads, tq, LANE), lambda b, h, t, qi, kj: (b, h, qi[t], 0)),
            pl.BlockSpec((1, heads, tk, LANE), lambda b, h, t, qi, kj: (b, h, kj[t], 0)),
            pl.BlockSpec((1, tk, width), lambda b, h, t, qi, kj: (b, kj[t], h)),
        ],
        out_specs=pl.BlockSpec((1, tq, width), lambda b, h, t, qi, kj: (b, qi[t], h)),
        scratch_shapes=[
            pltpu.VMEM((heads, tq, LANE), F32),
            pltpu.VMEM((heads, tq, LANE), F32),
        ],
    )
    return pl.pallas_call(
        functools.partial(_flash_kernel, tq=tq, tk=tk),
        grid_spec=grid_spec,
        out_shape=jax.ShapeDtypeStruct((batch, seq, MLA_WIDTH), BF16),
        compiler_params=_params("parallel", "parallel", "arbitrary"),
        name="mla_flash_attention",
    )(qi, kj, q, k, v.reshape(batch, seq, MLA_WIDTH))


def _merge_kernel(x_ref, ys_ref, yg_ref, ym_ref, gate_ref, ws_ref, wg_ref, wm_ref, wo_ref, g2_ref,
                  x1_ref, h2_ref):
    d = D_MODEL
    gate = _sigmoid(gate_ref[...].astype(F32))
    merged = (gate[:, 0:d] * jnp.dot(ys_ref[...], ws_ref[...], preferred_element_type=F32)
              + gate[:, d:2 * d] * jnp.dot(yg_ref[...], wg_ref[...], preferred_element_type=F32)
              + gate[:, 2 * d:] * jnp.dot(ym_ref[...], wm_ref[...], preferred_element_type=F32))
    x1 = x_ref[...] + jnp.dot(merged.astype(BF16), wo_ref[...], preferred_element_type=F32)
    x1_ref[...] = x1
    ms = jnp.mean(x1 * x1, axis=-1, keepdims=True)
    h2_ref[...] = (x1 * lax.rsqrt(ms + EPS) * g2_ref[...]).astype(h2_ref.dtype)


def merge(x, y_ssd, y_sgu, y_mla, gate, w_ssd, w_sgu, w_mla, w_out, ffn_norm_g, tm, h2_dtype):
    t, d = x.shape
    row = lambda i: (i, 0)
    const = lambda i: (0, 0)
    act = pl.BlockSpec((tm, d), row)
    wspec = pl.BlockSpec((d, d), const)
    return pl.pallas_call(
        _merge_kernel,
        grid=(t // tm,),
        in_specs=[act, act, act, act, pl.BlockSpec((tm, 3 * d), row), wspec, wspec, wspec, wspec,
                  pl.BlockSpec((1, d), const)],
        out_specs=[act, act],
        out_shape=[jax.ShapeDtypeStruct((t, d), F32), jax.ShapeDtypeStruct((t, d), h2_dtype)],
        compiler_params=_params("parallel"),
        name="merge_out_proj",
    )(x, y_ssd, y_sgu, y_mla, gate, w_ssd.astype(BF16), w_sgu.astype(BF16), w_mla.astype(BF16),
      w_out.astype(BF16), ffn_norm_g.reshape(1, d))


def _ffn_kernel(h_ref, x_ref, wg_ref, wu_ref, wd_ref, o_ref, *, tf):
    h = h_ref[...]
    acc = x_ref[...]
    for c in range(wg_ref.shape[1] // tf):
        a = jnp.dot(h, wg_ref[:, c * tf:(c + 1) * tf], preferred_element_type=F32)
        b = jnp.dot(h, wu_ref[:, c * tf:(c + 1) * tf], preferred_element_type=F32)
        act = ((a * _sigmoid(a)) * b).astype(BF16)
        acc = acc + jnp.dot(act, wd_ref[c * tf:(c + 1) * tf, :], preferred_element_type=F32)
    o_ref[...] = acc


def ffn(h, x, wg, wu, wd, tm, tf):
    t, d = x.shape
    f_dim = wg.shape[1]
    row = lambda i: (i, 0)
    const = lambda i: (0, 0)
    return pl.pallas_call(
        functools.partial(_ffn_kernel, tf=tf),
        grid=(t // tm,),
        in_specs=[
            pl.BlockSpec((tm, d), row),
            pl.BlockSpec((tm, d), row),
            pl.BlockSpec((d, f_dim), const),
            pl.BlockSpec((d, f_dim), const),
            pl.BlockSpec((f_dim, d), const),
        ],
        out_specs=pl.BlockSpec((tm, d), row),
        out_shape=jax.ShapeDtypeStruct((t, d), F32),
        compiler_params=_params("parallel"),
        name="swiglu_ffn",
    )(h, x, wg.astype(BF16), wu.astype(BF16), wd.astype(BF16))


def _router_kernel(x_ref, g_ref, rw_ref, c_ref):
    x = x_ref[...]
    h = x * lax.rsqrt(jnp.mean(x * x, axis=-1, keepdims=True) + EPS) * g_ref[...]
    h_hi = h.astype(BF16)
    h_lo = (h - h_hi.astype(F32)).astype(BF16)
    w = rw_ref[...]
    w_hi = w.astype(BF16)
    w_lo = (w - w_hi.astype(F32)).astype(BF16)
    logits = (jnp.dot(h_hi, w_hi, preferred_element_type=F32) + jnp.dot(h_lo, w_hi, preferred_element_type=F32)
              + jnp.dot(h_hi, w_lo, preferred_element_type=F32))
    lane = lax.broadcasted_iota(jnp.int32, logits.shape, 1)
    lg = jnp.where(lane < N_EXPERTS, logits, -jnp.inf)
    m1 = jnp.max(lg, axis=-1, keepdims=True)
    i1 = jnp.min(jnp.where(lg == m1, lane, LANE), axis=-1, keepdims=True)
    lg2 = jnp.where(lane == i1, -jnp.inf, lg)
    m2 = jnp.max(lg2, axis=-1, keepdims=True)
    i2 = jnp.min(jnp.where(lg2 == m2, lane, LANE), axis=-1, keepdims=True)
    e2 = jnp.exp(m2 - m1)
    w1 = 1.0 / (1.0 + e2)
    w2 = e2 / (1.0 + e2)
    c_ref[...] = (jnp.where(lane == ROUTE_E1, i1.astype(F32), 0.0) + jnp.where(lane == ROUTE_E2, i2.astype(F32), 0.0)
                  + jnp.where(lane == ROUTE_W1, w1, 0.0) + jnp.where(lane == ROUTE_W2, w2, 0.0))


def router(x, g, router_w, tm):
    t, d = x.shape
    rw = jnp.pad(router_w, ((0, 0), (0, LANE - N_EXPERTS)))
    return pl.pallas_call(
        _router_kernel,
        grid=(t // tm,),
        in_specs=[pl.BlockSpec((tm, d), lambda i: (i, 0)), pl.BlockSpec((1, d), lambda i: (0, 0)),
                  pl.BlockSpec((d, LANE), lambda i: (0, 0))],
        out_specs=pl.BlockSpec((tm, LANE), lambda i: (i, 0)),
        out_shape=jax.ShapeDtypeStruct((t, LANE), F32),
        compiler_params=_params("parallel"),
        name="moe_router",
    )(x, g.reshape(1, d), rw)


def _moe_plan(route, t, tm):
    n_assign = 2 * t
    n_tiles_max = n_assign // tm + N_EXPERTS
    e = jnp.concatenate([route[:, ROUTE_E1], route[:, ROUTE_E2]]).astype(jnp.int32)
    onehot = (e[:, None] == jnp.arange(N_EXPERTS, dtype=jnp.int32)[None, :]).astype(jnp.int32)
    csum = jnp.cumsum(onehot, axis=0)
    rank = jnp.sum(csum * onehot, axis=1) - 1
    counts = csum[-1]
    tiles_per = (counts + tm - 1) // tm
    tile_end = jnp.cumsum(tiles_per)
    n_tiles = tile_end[-1]
    pos = jnp.sum(onehot * (tile_end - tiles_per)[None, :], axis=1) * tm + rank
    tile = jnp.minimum(jnp.arange(n_tiles_max, dtype=jnp.int32), n_tiles - 1)
    tile_expert = jnp.sum((tile[:, None] >= tile_end[None, :]).astype(jnp.int32), axis=1)
    shape3 = (t // tm, 1, tm)
    return (tile_expert, tile.astype(jnp.int32), n_tiles.reshape(1).astype(jnp.int32),
            pos[:t].reshape(shape3), pos[t:].reshape(shape3))


def _moe_dispatch_kernel(pos1_ref, pos2_ref, h_ref, init_hbm, xs_hbm, stage, sem, *, tm):
    del init_hbm
    i = pl.program_id(0)
    last = pl.num_programs(0) - 1

    def row_copy(r, slot, parity):
        return pltpu.make_async_copy(stage.at[parity, pl.ds(r, 1), :], xs_hbm.at[pl.ds(slot, 1), :], sem.at[parity])

    def drain(parity):
        for _ in range(2):
            pltpu.make_async_copy(stage.at[parity], xs_hbm.at[pl.ds(0, tm), :], sem.at[parity]).wait()

    for parity in range(2):
        @pl.when(i % 2 == parity)
        def _(parity=parity):
            stage[parity] = h_ref[...]
            for r in range(tm):
                row_copy(r, pos1_ref[0, 0, r], parity).start(priority=0)
                row_copy(r, pos2_ref[0, 0, r], parity).start(priority=1)

            @pl.when(i >= 1)
            def _():
                drain(1 - parity)

            @pl.when(i == last)
            def _():
                drain(parity)


def moe_dispatch(h, pos1, pos2, n_slots, tm):
    t, d = h.shape
    smem_tile = pl.BlockSpec((1, 1, tm), lambda i: (i, 0, 0), memory_space=pltpu.SMEM)
    return pl.pallas_call(
        functools.partial(_moe_dispatch_kernel, tm=tm),
        grid=(t // tm,),
        in_specs=[smem_tile, smem_tile, pl.BlockSpec((tm, d), lambda i: (i, 0)), pl.BlockSpec(memory_space=pl.ANY)],
        out_specs=pl.BlockSpec(memory_space=pl.ANY),
        out_shape=jax.ShapeDtypeStruct((n_slots, d), h.dtype),
        scratch_shapes=[pltpu.VMEM((2, tm, d), h.dtype), pltpu.SemaphoreType.DMA((2,))],
        input_output_aliases={3: 0},
        compiler_params=_params("arbitrary"),
        name="moe_dispatch",
    )(pos1, pos2, h, jnp.zeros((n_slots, d), h.dtype))


def _moe_group_kernel(texp_ref, tile_ref, ntiles_ref, x_ref, wg_hbm, wu_hbm, wd_hbm, y_ref,
                      wg_s, wu_s, wd_s, stage_in, stage_out, sem, *, tf):
    del tile_ref
    i = pl.program_id(0)
    f_dim = wg_s.shape[1]
    n_chunk = f_dim // tf
    used = i < ntiles_ref[0]
    expert = texp_ref[i]
    new_expert = (i == 0) | (expert != texp_ref[jnp.maximum(i - 1, 0)])

    @pl.when(jnp.logical_not(used))
    def _():
        y_ref[...] = jnp.zeros_like(y_ref)

    def chunk_copy(k):
        c = k % n_chunk
        slot = k % 2
        if k < 2 * n_chunk:
            src = (wg_hbm if k < n_chunk else wu_hbm).at[expert, :, pl.ds(c * tf, tf)]
            return pltpu.make_async_copy(src, stage_in.at[slot], sem.at[slot])
        return pltpu.make_async_copy(wd_hbm.at[expert, pl.ds(c * tf, tf), :], stage_out.at[slot], sem.at[slot])

    @pl.when(used & new_expert)
    def _():
        chunk_copy(0).start()
        for k in range(3 * n_chunk):
            if k + 1 < 3 * n_chunk:
                chunk_copy(k + 1).start()
            chunk_copy(k).wait()
            c = k % n_chunk
            if k < n_chunk:
                wg_s[:, c * tf:(c + 1) * tf] = stage_in[k % 2].astype(BF16)
            elif k < 2 * n_chunk:
                wu_s[:, c * tf:(c + 1) * tf] = stage_in[k % 2].astype(BF16)
            else:
                wd_s[c * tf:(c + 1) * tf, :] = stage_out[k % 2].astype(BF16)

    @pl.when(used)
    def _():
        h = x_ref[...].astype(BF16)
        acc = jnp.zeros(y_ref.shape, F32)
        for c in range(n_chunk):
            a = jnp.dot(h, wg_s[:, c * tf:(c + 1) * tf], preferred_element_type=F32)
            b = jnp.dot(h, wu_s[:, c * tf:(c + 1) * tf], preferred_element_type=F32)
            act = ((a * _sigmoid(a)) * b).astype(BF16)
            acc = acc + jnp.dot(act, wd_s[c * tf:(c + 1) * tf, :], preferred_element_type=F32)
        y_ref[...] = acc


def moe_grouped(xs, tile_expert, tile_block, n_tiles, wg, wu, wd, tm, tf):
    n_slots, d = xs.shape
    f_dim = wg.shape[2]
    row_spec = pl.BlockSpec((tm, d), lambda i, te, tb, nt: (tb[i], 0))
    hbm_spec = pl.BlockSpec(memory_space=pl.ANY)
    grid_spec = pltpu.PrefetchScalarGridSpec(
        num_scalar_prefetch=3,
        grid=(n_slots // tm,),
        in_specs=[row_spec, hbm_spec, hbm_spec, hbm_spec],
        out_specs=pl.BlockSpec((tm, d), lambda i, te, tb, nt: (i, 0)),
        scratch_shapes=[
            pltpu.VMEM((d, f_dim), BF16), pltpu.VMEM((d, f_dim), BF16), pltpu.VMEM((f_dim, d), BF16),
            pltpu.VMEM((2, d, tf), wg.dtype), pltpu.VMEM((2, tf, d), wd.dtype),
            pltpu.SemaphoreType.DMA((2,)),
        ],
    )
    return pl.pallas_call(
        functools.partial(_moe_group_kernel, tf=tf),
        grid_spec=grid_spec,
        out_shape=jax.ShapeDtypeStruct((n_slots, d), F32),
        compiler_params=pltpu.CompilerParams(dimension_semantics=("arbitrary",),
                                             vmem_limit_bytes=MOE_VMEM_LIMIT_BYTES),
        name="moe_grouped_swiglu",
    )(tile_expert, tile_block, n_tiles, xs, wg, wu, wd)


def _moe_combine_kernel(pos1_ref, pos2_ref, pos1_next_ref, pos2_next_ref, x_ref, r_ref, y_hbm, o_ref, gbuf, sem,
                        *, tm):
    i = pl.program_id(0)
    last = pl.num_programs(0) - 1
    cur = i % 2

    def start_gather(p1_ref, p2_ref, buf):
        for r in range(tm):
            pltpu.make_async_copy(y_hbm.at[pl.ds(p1_ref[0, 0, r], 1), :], gbuf.at[buf, pl.ds(r, 1), :],
                                  sem.at[buf]).start(priority=0)
            pltpu.make_async_copy(y_hbm.at[pl.ds(p2_ref[0, 0, r], 1), :], gbuf.at[buf, pl.ds(tm + r, 1), :],
                                  sem.at[buf]).start(priority=1)

    @pl.when(i == 0)
    def _():
        start_gather(pos1_ref, pos2_ref, 0)

    for parity in range(2):
        @pl.when((i < last) & (cur == parity))
        def _(parity=parity):
            start_gather(pos1_next_ref, pos2_next_ref, 1 - parity)

    pltpu.make_async_copy(y_hbm.at[pl.ds(0, 2 * tm), :], gbuf.at[cur], sem.at[cur]).wait()
    r = r_ref[...]
    o_ref[...] = (x_ref[...] + r[:, ROUTE_W1:ROUTE_W1 + 1] * gbuf[cur, 0:tm, :]
                  + r[:, ROUTE_W2:ROUTE_W2 + 1] * gbuf[cur, tm:2 * tm, :])


def moe_combine(x, route, y, pos1, pos2, tm):
    t, d = x.shape
    nb = t // tm
    smem_cur = pl.BlockSpec((1, 1, tm), lambda i: (i, 0, 0), memory_space=pltpu.SMEM)
    smem_next = pl.BlockSpec((1, 1, tm), lambda i: (jnp.minimum(i + 1, nb - 1), 0, 0), memory_space=pltpu.SMEM)
    return pl.pallas_call(
        functools.partial(_moe_combine_kernel, tm=tm),
        grid=(nb,),
        in_specs=[smem_cur, smem_cur, smem_next, smem_next,
                  pl.BlockSpec((tm, d), lambda i: (i, 0)), pl.BlockSpec((tm, LANE), lambda i: (i, 0)),
                  pl.BlockSpec(memory_space=pl.ANY)],
        out_specs=pl.BlockSpec((tm, d), lambda i: (i, 0)),
        out_shape=jax.ShapeDtypeStruct((t, d), F32),
        scratch_shapes=[pltpu.VMEM((2, 2 * tm, d), F32), pltpu.SemaphoreType.DMA((2,))],
        compiler_params=_params("arbitrary"),
        name="moe_combine",
    )(pos1, pos2, pos1, pos2, x, route, y)


def _pack_in_proj(w):
    o = 0
    z = w[:, o:o + SSD_INNER]; o += SSD_INNER
    xbc = w[:, o:o + SSD_CONV_CH]; o += SSD_CONV_CH
    dt = w[:, o:o + SSD_HEADS]; o += SSD_HEADS
    uv = w[:, o:o + 2 * SGU_WIDTH]; o += 2 * SGU_WIDTH
    q_lat = w[:, o:o + MLA_Q_RANK]; o += MLA_Q_RANK
    kv_lat = w[:, o:o + MLA_KV_RANK]; o += MLA_KV_RANK
    k_pe = w[:, o:o + MLA_ROPE]; o += MLA_ROPE
    gate = w[:, o:]
    d = w.shape[0]
    zeros = lambda n: jnp.zeros((d, n), w.dtype)
    small = jnp.concatenate([
        q_lat, kv_lat,
        zeros(MLA_NOPE), k_pe, zeros(LANE - MLA_QK),
        zeros(MLA_NOPE), _swap_rope_halves(k_pe), zeros(LANE - MLA_QK),
        dt, zeros(LANE - SSD_HEADS)], axis=1)
    return (jnp.concatenate([z, xbc], axis=1).astype(BF16), uv.astype(BF16), gate.astype(BF16),
            small.astype(BF16))


def _pick(n, prefs):
    for p in prefs:
        if n % p == 0:
            return p
    return n


def kernel(x, mix_norm_g, w_in, conv_w, conv_b, dt_bias, a_log, d_skip, ssd_norm_g, sgu_ln_g, sgu_ln_b, sgu_w,
           sgu_b, q_norm_g, w_uq, kv_norm_g, w_ukv, q_head_g, k_head_g, w_br_ssd, w_br_sgu, w_br_mla, w_out,
           ffn_norm_g, ffn_w_gate, ffn_w_up, ffn_w_down, router_w, moe_w_gate, moe_w_up, moe_w_down):
    batch, seq, d = x.shape
    t = batch * seq
    depth = w_in.shape[0]
    tm = _pick(t, (1024, 512, 256, 128))
    xf = x.reshape(t, d)
    for i in range(depth):
        w_zx, w_uv, w_gate, w_small = _pack_in_proj(w_in[i])
        hn = rmsnorm(xf, mix_norm_g[i], tm)
        zx = matmul(hn, w_zx, BF16, tm, _pick(w_zx.shape[1], (1280,)), "in_proj_ssd")
        uv = matmul(hn, w_uv, BF16, tm, _pick(w_uv.shape[1], (1024,)), "in_proj_sgu")
        gate = matmul(hn, w_gate, BF16, tm, _pick(w_gate.shape[1], (1536,)), "in_proj_gate")
        small = matmul(hn, w_small, F32, tm, SMALL_WIDTH, "in_proj_small")
        y_ssd = ssd_branch(zx, small, conv_w[i], conv_b[i], dt_bias[i], a_log[i], d_skip[i], ssd_norm_g[i],
                           batch, seq, _pick(seq // SSD_CHUNK, (4, 2, 1)))
        y_sgu = sgu_branch(uv, sgu_ln_g[i], sgu_ln_b[i], sgu_w[i], sgu_b[i], _pick(seq // SGU_CHUNK, (4, 2, 1)))
        q, k, v = mla_prep(small, q_norm_g[i], w_uq[i], kv_norm_g[i], w_ukv[i], q_head_g[i], k_head_g[i],
                           batch, seq, _pick(seq, (512, 256, 128)))
        y_mla = flash_attention(q, k, v, batch, seq, _pick(seq, FLASH_TQ), _pick(seq, FLASH_TK)).reshape(t, MLA_WIDTH)
        is_moe = i % 2 == 1
        x1, h2 = merge(xf, y_ssd, y_sgu, y_mla, gate, w_br_ssd[i], w_br_sgu[i], w_br_mla[i], w_out[i],
                       ffn_norm_g[i], _pick(t, (512, 256, 128)), F32 if is_moe else BF16)
        j = i // 2
        if not is_moe:
            xf = ffn(h2, x1, ffn_w_gate[j], ffn_w_up[j], ffn_w_down[j], _pick(t, (512, 256, 128)),
                     _pick(ffn_w_gate.shape[2], (256, 128)))
        else:
            route = router(x1, ffn_norm_g[i], router_w[j], tm)
            tm_e = _pick(t, (512, 256, 128))
            tile_expert, tile_block, n_tiles, pos1, pos2 = _moe_plan(route, t, tm_e)
            xs = moe_dispatch(h2, pos1, pos2, (2 * t // tm_e + N_EXPERTS) * tm_e, tm_e)
            ys = moe_grouped(xs, tile_expert, tile_block, n_tiles, moe_w_gate[j], moe_w_up[j], moe_w_down[j],
                             tm_e, _pick(moe_w_gate.shape[3], (512, 256)))
            xf = moe_combine(x1, route, ys, pos1, pos2, tm_e)
    return xf.reshape(batch, seq, d)
```

```python
import functools
import math

import jax
import jax.numpy as jnp
from jax import lax
from jax.experimental import pallas as pl
from jax.experimental.pallas import tpu as pltpu

F32 = jnp.float32
BF16 = jnp.bfloat16

D_MODEL = 1024
SSD_HEADS = 16
SSD_HEAD_DIM = 64
SSD_INNER = SSD_HEADS * SSD_HEAD_DIM
SSD_GROUPS = 2
SSD_STATE = 128
SSD_CONV = 4
SSD_CHUNK = 128
SSD_CONV_CH = SSD_INNER + 2 * SSD_GROUPS * SSD_STATE
SGU_WIDTH = 1024
SGU_GROUPS = 8
SGU_CHUNK = 128
MLA_HEADS = 16
MLA_Q_RANK = 256
MLA_KV_RANK = 128
MLA_NOPE = 64
MLA_ROPE = 32
MLA_V = 64
MLA_QK = MLA_NOPE + MLA_ROPE
MLA_WIDTH = MLA_HEADS * MLA_V
ROPE_BASE = 10000.0
N_EXPERTS = 8
EPS = 1e-6
LN_EPS = 1e-5

LANE = 128
SUBLANE = 8
VMEM_LIMIT_BYTES = 48 * 1024 * 1024
MOE_VMEM_LIMIT_BYTES = 56 * 1024 * 1024

ROUTE_E1, ROUTE_E2, ROUTE_W1, ROUTE_W2 = 0, 1, 2, 3

SMALL_KPE_TILE = 3
SMALL_KPEP_TILE = 4
SMALL_DT_TILE = 5
SMALL_WIDTH = 6 * LANE


def _params(*semantics):
    return pltpu.CompilerParams(dimension_semantics=semantics, vmem_limit_bytes=VMEM_LIMIT_BYTES)


def _sigmoid(x):
    return 1.0 / (1.0 + jnp.exp(-x))


def _rmsnorm_kernel(x_ref, g_ref, o_ref):
    x = x_ref[...]
    ms = jnp.mean(x * x, axis=-1, keepdims=True)
    o_ref[...] = (x * lax.rsqrt(ms + EPS) * g_ref[...]).astype(o_ref.dtype)


def rmsnorm(x, g, tm):
    t, d = x.shape
    return pl.pallas_call(
        _rmsnorm_kernel,
        grid=(t // tm,),
        in_specs=[pl.BlockSpec((tm, d), lambda i: (i, 0)), pl.BlockSpec((1, d), lambda i: (0, 0))],
        out_specs=pl.BlockSpec((tm, d), lambda i: (i, 0)),
        out_shape=jax.ShapeDtypeStruct((t, d), BF16),
        compiler_params=_params("parallel"),
        name="rmsnorm",
    )(x, g.reshape(1, d))


def _matmul_kernel(a_ref, w_ref, o_ref):
    o_ref[...] = jnp.dot(a_ref[...], w_ref[...], preferred_element_type=F32).astype(o_ref.dtype)


def matmul(a, w, out_dtype, tm, tn, name):
    t, k = a.shape
    n = w.shape[1]
    return pl.pallas_call(
        _matmul_kernel,
        grid=(t // tm, n // tn),
        in_specs=[pl.BlockSpec((tm, k), lambda i, j: (i, 0)), pl.BlockSpec((k, tn), lambda i, j: (0, j))],
        out_specs=pl.BlockSpec((tm, tn), lambda i, j: (i, j)),
        out_shape=jax.ShapeDtypeStruct((t, n), out_dtype),
        compiler_params=_params("parallel", "parallel"),
        name=name,
    )(a, w)


def _ssd_kernel(zx_ref, dt_ref, convw_ref, convb_ref, dtb_ref, alog_ref, dskip_ref, ng_ref, exp_ref,
                o_ref, tail_ref, st_ref, *, chunks_per_step):
    l = SSD_CHUNK
    half = SSD_INNER // SSD_GROUPS

    @pl.when(pl.program_id(1) == 0)
    def _():
        tail_ref[...] = jnp.zeros_like(tail_ref)
        st_ref[...] = jnp.zeros_like(st_ref)

    row = lax.broadcasted_iota(jnp.int32, (l, l), 0)
    lane = lax.broadcasted_iota(jnp.int32, (l, l), 1)
    causal = row >= lane
    low_half = lane < SSD_HEAD_DIM
    tri = causal.astype(F32)
    a_neg = -jnp.exp(alog_ref[...])
    shifts = [(lane == row - (SSD_CONV - 1 - j)).astype(BF16) for j in range(SSD_CONV - 1)]

    for ci in range(chunks_per_step):
        r0 = ci * l
        z = zx_ref[r0:r0 + l, 0:SSD_INNER].astype(F32)
        cur = zx_ref[r0:r0 + l, SSD_INNER:SSD_INNER + SSD_CONV_CH]
        conv = convb_ref[...] + convw_ref[SSD_CONV - 1:SSD_CONV, :] * cur.astype(F32)
        head = jnp.zeros((SUBLANE, SSD_CONV_CH), F32)
        for j in range(SSD_CONV - 1):
            d = SSD_CONV - 1 - j
            conv = conv + convw_ref[j:j + 1, :] * jnp.dot(shifts[j], cur, preferred_element_type=F32)
            head = head + convw_ref[j:j + 1, :] * tail_ref[SUBLANE - d:2 * SUBLANE - d, :]
        conv = jnp.concatenate([conv[0:SUBLANE] + head, conv[SUBLANE:]], axis=0)
        tail_ref[0:SUBLANE, :] = cur[l - SUBLANE:l, :].astype(F32)
        xbc = conv * _sigmoid(conv)
        xh = xbc[:, 0:SSD_INNER]
        bm = xbc[:, SSD_INNER:SSD_INNER + SSD_GROUPS * SSD_STATE].astype(BF16)
        cm = xbc[:, SSD_INNER + SSD_GROUPS * SSD_STATE:].astype(BF16)

        dt_in = dt_ref[r0:r0 + l, :] + dtb_ref[...]
        dt = jnp.maximum(dt_in, 0.0) + jnp.log1p(jnp.exp(-jnp.abs(dt_in)))
        adt = dt * a_neg
        cs = jnp.dot(tri, adt, preferred_element_type=F32, precision=lax.Precision.HIGHEST)
        cs_t = cs.T
        cs_last = cs[l - 1:l, :]
        small = jnp.concatenate([dt, jnp.exp(cs_last - cs), jnp.exp(cs)], axis=0).astype(BF16)
        wide = jnp.dot(small, exp_ref[...], preferred_element_type=F32)
        dt_e = wide[0:l]
        dstate_e = wide[l:2 * l]
        ecs_e = wide[2 * l:3 * l]
        cdec_e = ecs_e[l - 1:l, :]

        xs = xh * dt_e
        xs_b = xs.astype(BF16)
        xd_b = (xs * dstate_e).astype(BF16)

        y_parts = []
        off_parts = []
        for g in range(SSD_GROUPS):
            bg = bm[:, g * SSD_STATE:(g + 1) * SSD_STATE]
            cg = cm[:, g * SSD_STATE:(g + 1) * SSD_STATE]
            cb = lax.dot_general(cg, bg, (((1,), (1,)), ((), ())), preferred_element_type=F32)
            for hp in range(SSD_HEADS // SSD_GROUPS // 2):
                pair = g * (SSD_HEADS // SSD_GROUPS // 2) + hp
                xs_pair = xs_b[:, pair * LANE:(pair + 1) * LANE]
                ys = []
                for h in (2 * pair, 2 * pair + 1):
                    diff = cs[:, h:h + 1] - cs_t[h:h + 1, :]
                    decay = jnp.exp(jnp.where(causal, diff, -jnp.inf))
                    ys.append(jnp.dot((cb * decay).astype(BF16), xs_pair, preferred_element_type=F32))
                y_parts.append(jnp.where(low_half, ys[0], ys[1]))
            st_g = st_ref[:, g * half:(g + 1) * half]
            off_parts.append(jnp.dot(cg, st_g.astype(BF16), preferred_element_type=F32))
            new = lax.dot_general(bg, xd_b[:, g * half:(g + 1) * half], (((0,), (0,)), ((), ())),
                                  preferred_element_type=F32)
            st_ref[:, g * half:(g + 1) * half] = st_g * cdec_e[:, g * half:(g + 1) * half] + new

        y = (jnp.concatenate(y_parts, axis=1) + jnp.concatenate(off_parts, axis=1) * ecs_e
             + xh * dskip_ref[...])
        yg = y * (z * _sigmoid(z))
        outs = []
        for g in range(SSD_GROUPS):
            yh = yg[:, g * half:(g + 1) * half]
            ms = jnp.mean(yh * yh, axis=-1, keepdims=True)
            outs.append(yh * lax.rsqrt(ms + EPS))
        o_ref[r0:r0 + l, :] = (jnp.concatenate(outs, axis=1) * ng_ref[...]).astype(o_ref.dtype)


def ssd_branch(zx, small, conv_w, conv_b, dt_bias, a_log, d_skip, norm_g, batch, seq, chunks_per_step):
    rows = chunks_per_step * SSD_CHUNK
    steps = seq // rows
    pad_h = LANE - SSD_HEADS
    dtb = jnp.pad(dt_bias, (0, pad_h)).reshape(1, LANE)
    alog = jnp.pad(a_log, (0, pad_h)).reshape(1, LANE)
    dskip = jnp.repeat(d_skip, SSD_HEAD_DIM).reshape(1, SSD_INNER)
    expand = (jnp.arange(LANE)[:, None] == jnp.arange(SSD_INNER)[None, :] // SSD_HEAD_DIM).astype(BF16)
    const = lambda b, c: (0, 0)
    return pl.pallas_call(
        functools.partial(_ssd_kernel, chunks_per_step=chunks_per_step),
        grid=(batch, steps),
        in_specs=[
            pl.BlockSpec((rows, SSD_INNER + SSD_CONV_CH), lambda b, c: (b * steps + c, 0)),
            pl.BlockSpec((rows, LANE), lambda b, c: (b * steps + c, SMALL_DT_TILE)),
            pl.BlockSpec((SSD_CONV, SSD_CONV_CH), const),
            pl.BlockSpec((1, SSD_CONV_CH), const),
            pl.BlockSpec((1, LANE), const),
            pl.BlockSpec((1, LANE), const),
            pl.BlockSpec((1, SSD_INNER), const),
            pl.BlockSpec((1, SSD_INNER), const),
            pl.BlockSpec((LANE, SSD_INNER), const),
        ],
        out_specs=pl.BlockSpec((rows, SSD_INNER), lambda b, c: (b * steps + c, 0)),
        out_shape=jax.ShapeDtypeStruct((batch * seq, SSD_INNER), BF16),
        scratch_shapes=[
            pltpu.VMEM((2 * SUBLANE, SSD_CONV_CH), F32),
            pltpu.VMEM((SSD_STATE, SSD_INNER), F32),
        ],
        compiler_params=_params("parallel", "arbitrary"),
        name="ssd_branch",
    )(zx, small, conv_w, conv_b.reshape(1, -1), dtb, alog, dskip, norm_g.reshape(1, -1), expand)


def _sgu_kernel(uv_ref, g_ref, b_ref, w_ref, bias_ref, o_ref, *, chunks_per_step):
    l = SGU_CHUNK
    uv = uv_ref[...].astype(F32)
    act = 0.5 * uv * (1.0 + lax.erf(uv * (1.0 / math.sqrt(2.0))))
    u = act[:, 0:SGU_WIDTH]
    v = act[:, SGU_WIDTH:]
    mu = jnp.mean(v, axis=-1, keepdims=True)
    vc = v - mu
    var = jnp.mean(vc * vc, axis=-1, keepdims=True)
    vn = (vc * lax.rsqrt(var + LN_EPS) * g_ref[...] + b_ref[...]).astype(BF16)
    row = lax.broadcasted_iota(jnp.int32, (l, l), 0)
    col = lax.broadcasted_iota(jnp.int32, (l, l), 1)
    causal = row >= col
    for g in range(SGU_GROUPS):
        wg = jnp.where(causal, w_ref[g], 0.0).astype(BF16)
        c0 = g * LANE
        for ci in range(chunks_per_step):
            r0 = ci * l
            mixed = jnp.dot(wg, vn[r0:r0 + l, c0:c0 + LANE], preferred_element_type=F32)
            o_ref[r0:r0 + l, c0:c0 + LANE] = (
                u[r0:r0 + l, c0:c0 + LANE] * (mixed + bias_ref[:, c0:c0 + LANE])).astype(o_ref.dtype)


def sgu_branch(uv, ln_g, ln_b, w_s, b_s, chunks_per_step):
    t = uv.shape[0]
    rows = chunks_per_step * SGU_CHUNK
    bias = jnp.repeat(b_s.T, SGU_WIDTH // SGU_GROUPS, axis=1)
    const2 = lambda i: (0, 0)
    return pl.pallas_call(
        functools.partial(_sgu_kernel, chunks_per_step=chunks_per_step),
        grid=(t // rows,),
        in_specs=[
            pl.BlockSpec((rows, 2 * SGU_WIDTH), lambda i: (i, 0)),
            pl.BlockSpec((1, SGU_WIDTH), const2),
            pl.BlockSpec((1, SGU_WIDTH), const2),
            pl.BlockSpec((SGU_GROUPS, SGU_CHUNK, SGU_CHUNK), lambda i: (0, 0, 0)),
            pl.BlockSpec((SGU_CHUNK, SGU_WIDTH), const2),
        ],
        out_specs=pl.BlockSpec((rows, SGU_WIDTH), lambda i: (i, 0)),
        out_shape=jax.ShapeDtypeStruct((t, SGU_WIDTH), BF16),
        compiler_params=_params("parallel"),
        name="sgu_branch",
    )(uv, ln_g.reshape(1, -1), ln_b.reshape(1, -1), w_s, bias)


def _mla_prep_kernel(small_ref, cos_ref, sin_ref, qng_ref, kvng_ref, qhg_ref, khg_ref, wq_ref, wqp_ref, wk_ref,
                     wv_ref, q_ref, k_ref, v_ref):
    sm = small_ref[...]
    q_lat = sm[:, 0:MLA_Q_RANK]
    kv_lat = sm[:, MLA_Q_RANK:MLA_Q_RANK + MLA_KV_RANK]
    kpe = sm[:, SMALL_KPE_TILE * LANE:(SMALL_KPE_TILE + 1) * LANE]
    kpe_p = sm[:, SMALL_KPEP_TILE * LANE:(SMALL_KPEP_TILE + 1) * LANE]

    def rms(x, g):
        return x * lax.rsqrt(jnp.mean(x * x, axis=-1, keepdims=True) + EPS) * g

    ql = rms(q_lat, qng_ref[...]).astype(BF16)
    kvl = rms(kv_lat, kvng_ref[...]).astype(BF16)
    v_ref[...] = jnp.dot(kvl, wv_ref[...], preferred_element_type=F32).astype(v_ref.dtype)

    pair_w = 2 * LANE
    same_head = ((lax.broadcasted_iota(jnp.int32, (pair_w, pair_w), 0) < LANE)
                 == (lax.broadcasted_iota(jnp.int32, (pair_w, pair_w), 1) < LANE))
    ones_pair = same_head.astype(BF16)

    def head_inv_rms(x):
        sq = (x * x).astype(BF16)
        ssq = jnp.concatenate(
            [jnp.dot(sq[:, p * pair_w:(p + 1) * pair_w], ones_pair, preferred_element_type=F32)
             for p in range(MLA_HEADS // 2)], axis=1)
        return lax.rsqrt(ssq * (1.0 / MLA_QK) + EPS)

    cos = cos_ref[...]
    sin = sin_ref[...]
    rep = lambda a: jnp.tile(a, (1, MLA_HEADS))

    q_scale = MLA_QK ** -0.5 * math.log2(math.e)
    q_x = jnp.dot(ql, wq_ref[...], preferred_element_type=F32)
    q_xp = jnp.dot(ql, wqp_ref[...], preferred_element_type=F32)
    q_a = cos * qhg_ref[0:1, :] * q_scale
    q_b = sin * qhg_ref[1:2, :] * q_scale
    q_out = head_inv_rms(q_x) * (q_x * rep(q_a) + q_xp * rep(q_b))
    k_x = jnp.dot(kvl, wk_ref[...], preferred_element_type=F32) + rep(kpe)
    k_a = cos * khg_ref[0:1, :]
    k_b = sin * khg_ref[1:2, :]
    k_out = head_inv_rms(k_x) * (k_x * rep(k_a) + rep(kpe_p * k_b))
    for h in range(MLA_HEADS):
        q_ref[0, h] = q_out[:, h * LANE:(h + 1) * LANE].astype(q_ref.dtype)
        k_ref[0, h] = k_out[:, h * LANE:(h + 1) * LANE].astype(k_ref.dtype)


def _swap_rope_halves(a):
    half = MLA_ROPE // 2
    return jnp.concatenate([a[..., half:], a[..., :half]], axis=-1)


def _head_gain_rows(g):
    direct = jnp.pad(g, (0, LANE - MLA_QK))
    partner = jnp.pad(_swap_rope_halves(g[MLA_NOPE:]), (MLA_NOPE, LANE - MLA_QK))
    return jnp.stack([direct, partner])


def mla_prep(small, q_norm_g, w_uq, kv_norm_g, w_ukv, q_head_g, k_head_g, batch, seq, rows):
    steps = seq // rows
    w_uq3 = w_uq.reshape(MLA_Q_RANK, MLA_HEADS, MLA_QK)
    wq = jnp.pad(w_uq3, ((0, 0), (0, 0), (0, LANE - MLA_QK))).reshape(MLA_Q_RANK, -1).astype(BF16)
    wqp = jnp.pad(_swap_rope_halves(w_uq3[:, :, MLA_NOPE:]), ((0, 0), (0, 0), (MLA_NOPE, LANE - MLA_QK)))
    wqp = wqp.reshape(MLA_Q_RANK, -1).astype(BF16)
    w_ukv3 = w_ukv.reshape(MLA_KV_RANK, MLA_HEADS, MLA_NOPE + MLA_V)
    wk = jnp.pad(w_ukv3[:, :, :MLA_NOPE], ((0, 0), (0, 0), (0, LANE - MLA_NOPE))).reshape(MLA_KV_RANK, -1).astype(BF16)
    wv = w_ukv3[:, :, MLA_NOPE:].reshape(MLA_KV_RANK, MLA_WIDTH).astype(BF16)
    qhg = _head_gain_rows(q_head_g)
    khg = _head_gain_rows(k_head_g)
    inv = 1.0 / (ROPE_BASE ** (jnp.arange(0, MLA_ROPE, 2, dtype=F32) / MLA_ROPE))
    ang = jnp.arange(seq, dtype=F32)[:, None] * inv[None, :]
    c, s = jnp.cos(ang), jnp.sin(ang)
    ones = jnp.ones((seq, MLA_NOPE), F32)
    zeros = jnp.zeros((seq, MLA_NOPE), F32)
    tail1 = jnp.ones((seq, LANE - MLA_QK), F32)
    tail0 = jnp.zeros((seq, LANE - MLA_QK), F32)
    cos_t = jnp.concatenate([ones, c, c, tail1], axis=1)
    sin_t = jnp.concatenate([zeros, -s, s, tail0], axis=1)
    const = lambda b, i: (0, 0)
    qk_shape = jax.ShapeDtypeStruct((batch, MLA_HEADS, seq, LANE), BF16)
    return pl.pallas_call(
        _mla_prep_kernel,
        grid=(batch, steps),
        in_specs=[
            pl.BlockSpec((rows, SMALL_WIDTH), lambda b, i: (b * steps + i, 0)),
            pl.BlockSpec((rows, LANE), lambda b, i: (i, 0)),
            pl.BlockSpec((rows, LANE), lambda b, i: (i, 0)),
            pl.BlockSpec((1, MLA_Q_RANK), const),
            pl.BlockSpec((1, MLA_KV_RANK), const),
            pl.BlockSpec((2, LANE), const),
            pl.BlockSpec((2, LANE), const),
            pl.BlockSpec((MLA_Q_RANK, MLA_HEADS * LANE), const),
            pl.BlockSpec((MLA_Q_RANK, MLA_HEADS * LANE), const),
            pl.BlockSpec((MLA_KV_RANK, MLA_HEADS * LANE), const),
            pl.BlockSpec((MLA_KV_RANK, MLA_WIDTH), const),
        ],
        out_specs=[
            pl.BlockSpec((1, MLA_HEADS, rows, LANE), lambda b, i: (b, 0, i, 0)),
            pl.BlockSpec((1, MLA_HEADS, rows, LANE), lambda b, i: (b, 0, i, 0)),
            pl.BlockSpec((rows, MLA_WIDTH), lambda b, i: (b * steps + i, 0)),
        ],
        out_shape=[qk_shape, qk_shape, jax.ShapeDtypeStruct((batch * seq, MLA_WIDTH), BF16)],
        compiler_params=_params("parallel", "parallel"),
        name="mla_prep",
    )(small, cos_t, sin_t, q_norm_g.reshape(1, -1), kv_norm_g.reshape(1, -1), qhg, khg, wq, wqp, wk, wv)


FLASH_HEADS_PER_STEP = 8
FLASH_TQ = (1024, 512, 256, 128)
FLASH_TK = (512, 256, 128)


def _flash_kernel(qi_ref, kj_ref, q_ref, k_ref, v_ref, o_ref, m_ref, acc_ref, *, tq, tk):
    t = pl.program_id(2)
    qi = qi_ref[t]
    kj = kj_ref[t]
    heads = FLASH_HEADS_PER_STEP
    band = kj * tk - qi * tq

    @pl.when((kj == 0) & (band >= 0))
    def _():
        m_ref[...] = jnp.full_like(m_ref, -jnp.inf)
        acc_ref[...] = jnp.zeros_like(acc_ref)

    def step(lo, hi, triangular, first_block=False):
        rows = hi - lo
        first = lax.broadcasted_iota(jnp.int32, (tk, LANE), 1) < MLA_V
        one = jnp.ones((tk, LANE), BF16)
        if triangular:
            visible = (lax.broadcasted_iota(jnp.int32, (rows, tk), 1)
                       <= lax.broadcasted_iota(jnp.int32, (rows, tk), 0))
        for hh in range(heads):
            v = v_ref[0, :, (hh // 2) * LANE:(hh // 2 + 1) * LANE]
            v_ones = jnp.where(first, v, one) if hh % 2 == 0 else jnp.where(first, one, v)
            s = lax.dot_general(q_ref[0, hh, lo:hi, :], k_ref[0, hh], (((1,), (1,)), ((), ())),
                                preferred_element_type=F32)
            if triangular:
                s = jnp.where(visible, s, -jnp.inf)
            row_max = jnp.max(s, axis=1, keepdims=True)
            if first_block:
                m_new = jnp.broadcast_to(row_max, (rows, LANE))
                p = jnp.exp2(s - jnp.tile(m_new, (1, tk // LANE)))
                acc_ref[hh, lo:hi, :] = jnp.dot(p.astype(BF16), v_ones, preferred_element_type=F32)
            else:
                m_prev = m_ref[hh, lo:hi, :]
                m_new = jnp.maximum(m_prev, row_max)
                p = jnp.exp2(s - jnp.tile(m_new, (1, tk // LANE)))
                alpha = jnp.exp2(m_prev - m_new)
                acc_ref[hh, lo:hi, :] = (alpha * acc_ref[hh, lo:hi, :]
                                         + jnp.dot(p.astype(BF16), v_ones, preferred_element_type=F32))
            m_ref[hh, lo:hi, :] = m_new

    def finalize():
        lane = lax.broadcasted_iota(jnp.int32, (tq, LANE), 1)
        for pair in range(heads // 2):
            a0 = acc_ref[2 * pair]
            a1 = acc_ref[2 * pair + 1]
            out = jnp.where(lane < MLA_V, a0 / pltpu.roll(a0, MLA_V, 1), a1 / pltpu.roll(a1, MLA_V, 1))
            o_ref[0, :, pair * LANE:(pair + 1) * LANE] = out.astype(o_ref.dtype)

    @pl.when((band < 0) & (kj == 0))
    def _():
        step(0, tq, False, first_block=True)

    @pl.when((band < 0) & (kj > 0))
    def _():
        step(0, tq, False)

    n_band = tq // tk
    for j in range(n_band):
        @pl.when(band == j * tk)
        def _(j=j):
            step(j * tk, (j + 1) * tk, True)
            if j + 1 < n_band:
                step((j + 1) * tk, tq, False)
            else:
                finalize()


def flash_attention(q, k, v, batch, seq, tq, tk):
    assert tq % tk == 0 and seq % tq == 0
    heads = FLASH_HEADS_PER_STEP
    width = heads * MLA_V
    pairs = [(i, j) for i in range(seq // tq) for j in range((i + 1) * tq // tk)]
    qi = jnp.asarray([p[0] for p in pairs], jnp.int32)
    kj = jnp.asarray([p[1] for p in pairs], jnp.int32)
    grid_spec = pltpu.PrefetchScalarGridSpec(
        num_scalar_prefetch=2,
        grid=(batch, MLA_HEADS // heads, len(pairs)),
        in_specs=[
            pl.BlockSpec((1, heads, tq, LANE), lambda b, h, t, qi, kj: (b, h, qi[t], 0)),
            pl.BlockSpec((1, heads, tk, LANE), lambda b, h, t, qi, kj: (b, h, kj[t], 0)),
            pl.BlockSpec((1, tk, width), lambda b, h, t, qi, kj: (b, kj[t], h)),
        ],
        out_specs=pl.BlockSpec((1, tq, width), lambda b, h, t, qi, kj: (b, qi[t], h)),
        scratch_shapes=[
            pltpu.VMEM((heads, tq, LANE), F32),
            pltpu.VMEM((heads, tq, LANE), F32),
        ],
    )
    return pl.pallas_call(
        functools.partial(_flash_kernel, tq=tq, tk=tk),
        grid_spec=grid_spec,
        out_shape=jax.ShapeDtypeStruct((batch, seq, MLA_WIDTH), BF16),
        compiler_params=_params("parallel", "parallel", "arbitrary"),
        name="mla_flash_attention",
    )(qi, kj, q, k, v.reshape(batch, seq, MLA_WIDTH))


def _merge_kernel(x_ref, ys_ref, yg_ref, ym_ref, gate_ref, ws_ref, wg_ref, wm_ref, wo_ref, g2_ref,
                  x1_ref, h2_ref):
    d = D_MODEL
    gate = _sigmoid(gate_ref[...].astype(F32))
    merged = (gate[:, 0:d] * jnp.dot(ys_ref[...], ws_ref[...], preferred_element_type=F32)
              + gate[:, d:2 * d] * jnp.dot(yg_ref[...], wg_ref[...], preferred_element_type=F32)
              + gate[:, 2 * d:] * jnp.dot(ym_ref[...], wm_ref[...], preferred_element_type=F32))
    x1 = x_ref[...] + jnp.dot(merged.astype(BF16), wo_ref[...], preferred_element_type=F32)
    x1_ref[...] = x1
    ms = jnp.mean(x1 * x1, axis=-1, keepdims=True)
    h2_ref[...] = (x1 * lax.rsqrt(ms + EPS) * g2_ref[...]).astype(h2_ref.dtype)


def merge(x, y_ssd, y_sgu, y_mla, gate, w_ssd, w_sgu, w_mla, w_out, ffn_norm_g, tm, h2_dtype):
    t, d = x.shape
    row = lambda i: (i, 0)
    const = lambda i: (0, 0)
    act = pl.BlockSpec((tm, d), row)
    wspec = pl.BlockSpec((d, d), const)
    return pl.pallas_call(
        _merge_kernel,
        grid=(t // tm,),
        in_specs=[act, act, act, act, pl.BlockSpec((tm, 3 * d), row), wspec, wspec, wspec, wspec,
                  pl.BlockSpec((1, d), const)],
        out_specs=[act, act],
        out_shape=[jax.ShapeDtypeStruct((t, d), F32), jax.ShapeDtypeStruct((t, d), h2_dtype)],
        compiler_params=_params("parallel"),
        name="merge_out_proj",
    )(x, y_ssd, y_sgu, y_mla, gate, w_ssd.astype(BF16), w_sgu.astype(BF16), w_mla.astype(BF16),
      w_out.astype(BF16), ffn_norm_g.reshape(1, d))


def _ffn_kernel(h_ref, x_ref, wg_ref, wu_ref, wd_ref, o_ref, *, tf):
    h = h_ref[...]
    acc = x_ref[...]
    for c in range(wg_ref.shape[1] // tf):
        a = jnp.dot(h, wg_ref[:, c * tf:(c + 1) * tf], preferred_element_type=F32)
        b = jnp.dot(h, wu_ref[:, c * tf:(c + 1) * tf], preferred_element_type=F32)
        act = ((a * _sigmoid(a)) * b).astype(BF16)
        acc = acc + jnp.dot(act, wd_ref[c * tf:(c + 1) * tf, :], preferred_element_type=F32)
    o_ref[...] = acc


def ffn(h, x, wg, wu, wd, tm, tf):
    t, d = x.shape
    f_dim = wg.shape[1]
    row = lambda i: (i, 0)
    const = lambda i: (0, 0)
    return pl.pallas_call(
        functools.partial(_ffn_kernel, tf=tf),
        grid=(t // tm,),
        in_specs=[
            pl.BlockSpec((tm, d), row),
            pl.BlockSpec((tm, d), row),
            pl.BlockSpec((d, f_dim), const),
            pl.BlockSpec((d, f_dim), const),
            pl.BlockSpec((f_dim, d), const),
        ],
        out_specs=pl.BlockSpec((tm, d), row),
        out_shape=jax.ShapeDtypeStruct((t, d), F32),
        compiler_params=_params("parallel"),
        name="swiglu_ffn",
    )(h, x, wg.astype(BF16), wu.astype(BF16), wd.astype(BF16))


def _router_kernel(x_ref, g_ref, rw_ref, c_ref):
    x = x_ref[...]
    h = x * lax.rsqrt(jnp.mean(x * x, axis=-1, keepdims=True) + EPS) * g_ref[...]
    h_hi = h.astype(BF16)
    h_lo = (h - h_hi.astype(F32)).astype(BF16)
    w = rw_ref[...]
    w_hi = w.astype(BF16)
    w_lo = (w - w_hi.astype(F32)).astype(BF16)
    logits = (jnp.dot(h_hi, w_hi, preferred_element_type=F32) + jnp.dot(h_lo, w_hi, preferred_element_type=F32)
              + jnp.dot(h_hi, w_lo, preferred_element_type=F32))
    lane = lax.broadcasted_iota(jnp.int32, logits.shape, 1)
    lg = jnp.where(lane < N_EXPERTS, logits, -jnp.inf)
    m1 = jnp.max(lg, axis=-1, keepdims=True)
    i1 = jnp.min(jnp.where(lg == m1, lane, LANE), axis=-1, keepdims=True)
    lg2 = jnp.where(lane == i1, -jnp.inf, lg)
    m2 = jnp.max(lg2, axis=-1, keepdims=True)
    i2 = jnp.min(jnp.where(lg2 == m2, lane, LANE), axis=-1, keepdims=True)
    e2 = jnp.exp(m2 - m1)
    w1 = 1.0 / (1.0 + e2)
    w2 = e2 / (1.0 + e2)
    c_ref[...] = (jnp.where(lane == ROUTE_E1, i1.astype(F32), 0.0) + jnp.where(lane == ROUTE_E2, i2.astype(F32), 0.0)
                  + jnp.where(lane == ROUTE_W1, w1, 0.0) + jnp.where(lane == ROUTE_W2, w2, 0.0))


def router(x, g, router_w, tm):
    t, d = x.shape
    rw = jnp.pad(router_w, ((0, 0), (0, LANE - N_EXPERTS)))
    return pl.pallas_call(
        _router_kernel,
        grid=(t // tm,),
        in_specs=[pl.BlockSpec((tm, d), lambda i: (i, 0)), pl.BlockSpec((1, d), lambda i: (0, 0)),
                  pl.BlockSpec((d, LANE), lambda i: (0, 0))],
        out_specs=pl.BlockSpec((tm, LANE), lambda i: (i, 0)),
        out_shape=jax.ShapeDtypeStruct((t, LANE), F32),
        compiler_params=_params("parallel"),
        name="moe_router",
    )(x, g.reshape(1, d), rw)


def _moe_plan(route, t, tm):
    n_assign = 2 * t
    n_tiles_max = n_assign // tm + N_EXPERTS
    e = jnp.concatenate([route[:, ROUTE_E1], route[:, ROUTE_E2]]).astype(jnp.int32)
    onehot = (e[:, None] == jnp.arange(N_EXPERTS, dtype=jnp.int32)[None, :]).astype(jnp.int32)
    csum = jnp.cumsum(onehot, axis=0)
    rank = jnp.sum(csum * onehot, axis=1) - 1
    counts = csum[-1]
    tiles_per = (counts + tm - 1) // tm
    tile_end = jnp.cumsum(tiles_per)
    n_tiles = tile_end[-1]
    pos = jnp.sum(onehot * (tile_end - tiles_per)[None, :], axis=1) * tm + rank
    tile = jnp.minimum(jnp.arange(n_tiles_max, dtype=jnp.int32), n_tiles - 1)
    tile_expert = jnp.sum((tile[:, None] >= tile_end[None, :]).astype(jnp.int32), axis=1)
    shape3 = (t // tm, 1, tm)
    return (tile_expert, tile.astype(jnp.int32), n_tiles.reshape(1).astype(jnp.int32),
            pos[:t].reshape(shape3), pos[t:].reshape(shape3))


def _moe_dispatch_kernel(pos1_ref, pos2_ref, h_ref, init_hbm, xs_hbm, stage, sem, *, tm):
    del init_hbm
    i = pl.program_id(0)
    last = pl.num_programs(0) - 1

    def row_copy(r, slot, parity):
        return pltpu.make_async_copy(stage.at[parity, pl.ds(r, 1), :], xs_hbm.at[pl.ds(slot, 1), :], sem.at[parity])

    def drain(parity):
        for _ in range(2):
            pltpu.make_async_copy(stage.at[parity], xs_hbm.at[pl.ds(0, tm), :], sem.at[parity]).wait()

    for parity in range(2):
        @pl.when(i % 2 == parity)
        def _(parity=parity):
            stage[parity] = h_ref[...]
            for r in range(tm):
                row_copy(r, pos1_ref[0, 0, r], parity).start(priority=0)
                row_copy(r, pos2_ref[0, 0, r], parity).start(priority=1)

            @pl.when(i >= 1)
            def _():
                drain(1 - parity)

            @pl.when(i == last)
            def _():
                drain(parity)


def moe_dispatch(h, pos1, pos2, n_slots, tm):
    t, d = h.shape
    smem_tile = pl.BlockSpec((1, 1, tm), lambda i: (i, 0, 0), memory_space=pltpu.SMEM)
    return pl.pallas_call(
        functools.partial(_moe_dispatch_kernel, tm=tm),
        grid=(t // tm,),
        in_specs=[smem_tile, smem_tile, pl.BlockSpec((tm, d), lambda i: (i, 0)), pl.BlockSpec(memory_space=pl.ANY)],
        out_specs=pl.BlockSpec(memory_space=pl.ANY),
        out_shape=jax.ShapeDtypeStruct((n_slots, d), h.dtype),
        scratch_shapes=[pltpu.VMEM((2, tm, d), h.dtype), pltpu.SemaphoreType.DMA((2,))],
        input_output_aliases={3: 0},
        compiler_params=_params("arbitrary"),
        name="moe_dispatch",
    )(pos1, pos2, h, jnp.zeros((n_slots, d), h.dtype))


def _moe_group_kernel(texp_ref, tile_ref, ntiles_ref, x_ref, wg_hbm, wu_hbm, wd_hbm, y_ref,
                      wg_s, wu_s, wd_s, stage_in, stage_out, sem, *, tf):
    del tile_ref
    i = pl.program_id(0)
    f_dim = wg_s.shape[1]
    n_chunk = f_dim // tf
    used = i < ntiles_ref[0]
    expert = texp_ref[i]
    new_expert = (i == 0) | (expert != texp_ref[jnp.maximum(i - 1, 0)])

    @pl.when(jnp.logical_not(used))
    def _():
        y_ref[...] = jnp.zeros_like(y_ref)

    def piece_copy(c, kind):
        if kind == 2:
            return pltpu.make_async_copy(wd_hbm.at[expert, pl.ds(c * tf, tf), :], stage_out.at[c % 2],
                                         sem.at[2 + c % 2])
        src = (wg_hbm, wu_hbm)[kind].at[expert, :, pl.ds(c * tf, tf)]
        return pltpu.make_async_copy(src, stage_in.at[kind], sem.at[kind])

    @pl.when(used)
    def _():
        @pl.when(new_expert)
        def _():
            for kind in range(3):
                piece_copy(0, kind).start()

        h = x_ref[...].astype(BF16)
        acc = jnp.zeros(y_ref.shape, F32)
        for c in range(n_chunk):
            @pl.when(new_expert)
            def _(c=c):
                for kind in range(3):
                    piece_copy(c, kind).wait()
                    if kind == 0:
                        wg_s[:, c * tf:(c + 1) * tf] = stage_in[0].astype(BF16)
                    elif kind == 1:
                        wu_s[:, c * tf:(c + 1) * tf] = stage_in[1].astype(BF16)
                    else:
                        wd_s[c * tf:(c + 1) * tf, :] = stage_out[c % 2].astype(BF16)
                    if c + 1 < n_chunk:
                        piece_copy(c + 1, kind).start()

            a = jnp.dot(h, wg_s[:, c * tf:(c + 1) * tf], preferred_element_type=F32)
            b = jnp.dot(h, wu_s[:, c * tf:(c + 1) * tf], preferred_element_type=F32)
            act = ((a * _sigmoid(a)) * b).astype(BF16)
            acc = acc + jnp.dot(act, wd_s[c * tf:(c + 1) * tf, :], preferred_element_type=F32)
        y_ref[...] = acc


def moe_grouped(xs, tile_expert, tile_block, n_tiles, wg, wu, wd, tm, tf):
    n_slots, d = xs.shape
    f_dim = wg.shape[2]
    row_spec = pl.BlockSpec((tm, d), lambda i, te, tb, nt: (tb[i], 0))
    hbm_spec = pl.BlockSpec(memory_space=pl.ANY)
    grid_spec = pltpu.PrefetchScalarGridSpec(
        num_scalar_prefetch=3,
        grid=(n_slots // tm,),
        in_specs=[row_spec, hbm_spec, hbm_spec, hbm_spec],
        out_specs=pl.BlockSpec((tm, d), lambda i, te, tb, nt: (i, 0)),
        scratch_shapes=[
            pltpu.VMEM((d, f_dim), BF16), pltpu.VMEM((d, f_dim), BF16), pltpu.VMEM((f_dim, d), BF16),
            pltpu.VMEM((2, d, tf), wg.dtype), pltpu.VMEM((2, tf, d), wd.dtype),
            pltpu.SemaphoreType.DMA((4,)),
        ],
    )
    return pl.pallas_call(
        functools.partial(_moe_group_kernel, tf=tf),
        grid_spec=grid_spec,
        out_shape=jax.ShapeDtypeStruct((n_slots, d), F32),
        compiler_params=pltpu.CompilerParams(dimension_semantics=("arbitrary",),
                                             vmem_limit_bytes=MOE_VMEM_LIMIT_BYTES),
        name="moe_grouped_swiglu",
    )(tile_expert, tile_block, n_tiles, xs, wg, wu, wd)


def _moe_combine_kernel(pos1_ref, pos2_ref, pos1_next_ref, pos2_next_ref, x_ref, r_ref, y_hbm, o_ref, gbuf, sem,
                        *, tm):
    i = pl.program_id(0)
    last = pl.num_programs(0) - 1
    cur = i % 2

    def start_gather(p1_ref, p2_ref, buf):
        for r in range(tm):
            pltpu.make_async_copy(y_hbm.at[pl.ds(p1_ref[0, 0, r], 1), :], gbuf.at[buf, pl.ds(r, 1), :],
                                  sem.at[buf]).start(priority=0)
            pltpu.make_async_copy(y_hbm.at[pl.ds(p2_ref[0, 0, r], 1), :], gbuf.at[buf, pl.ds(tm + r, 1), :],
                                  sem.at[buf]).start(priority=1)

    @pl.when(i == 0)
    def _():
        start_gather(pos1_ref, pos2_ref, 0)

    for parity in range(2):
        @pl.when((i < last) & (cur == parity))
        def _(parity=parity):
            start_gather(pos1_next_ref, pos2_next_ref, 1 - parity)

    pltpu.make_async_copy(y_hbm.at[pl.ds(0, 2 * tm), :], gbuf.at[cur], sem.at[cur]).wait()
    r = r_ref[...]
    o_ref[...] = (x_ref[...] + r[:, ROUTE_W1:ROUTE_W1 + 1] * gbuf[cur, 0:tm, :]
                  + r[:, ROUTE_W2:ROUTE_W2 + 1] * gbuf[cur, tm:2 * tm, :])


def moe_combine(x, route, y, pos1, pos2, tm):
    t, d = x.shape
    nb = t // tm
    smem_cur = pl.BlockSpec((1, 1, tm), lambda i: (i, 0, 0), memory_space=pltpu.SMEM)
    smem_next = pl.BlockSpec((1, 1, tm), lambda i: (jnp.minimum(i + 1, nb - 1), 0, 0), memory_space=pltpu.SMEM)
    return pl.pallas_call(
        functools.partial(_moe_combine_kernel, tm=tm),
        grid=(nb,),
        in_specs=[smem_cur, smem_cur, smem_next, smem_next,
                  pl.BlockSpec((tm, d), lambda i: (i, 0)), pl.BlockSpec((tm, LANE), lambda i: (i, 0)),
                  pl.BlockSpec(memory_space=pl.ANY)],
        out_specs=pl.BlockSpec((tm, d), lambda i: (i, 0)),
        out_shape=jax.ShapeDtypeStruct((t, d), F32),
        scratch_shapes=[pltpu.VMEM((2, 2 * tm, d), F32), pltpu.SemaphoreType.DMA((2,))],
        compiler_params=_params("arbitrary"),
        name="moe_combine",
    )(pos1, pos2, pos1, pos2, x, route, y)


def _pack_in_proj(w):
    o = 0
    z = w[:, o:o + SSD_INNER]; o += SSD_INNER
    xbc = w[:, o:o + SSD_CONV_CH]; o += SSD_CONV_CH
    dt = w[:, o:o + SSD_HEADS]; o += SSD_HEADS
    uv = w[:, o:o + 2 * SGU_WIDTH]; o += 2 * SGU_WIDTH
    q_lat = w[:, o:o + MLA_Q_RANK]; o += MLA_Q_RANK
    kv_lat = w[:, o:o + MLA_KV_RANK]; o += MLA_KV_RANK
    k_pe = w[:, o:o + MLA_ROPE]; o += MLA_ROPE
    gate = w[:, o:]
    d = w.shape[0]
    zeros = lambda n: jnp.zeros((d, n), w.dtype)
    small = jnp.concatenate([
        q_lat, kv_lat,
        zeros(MLA_NOPE), k_pe, zeros(LANE - MLA_QK),
        zeros(MLA_NOPE), _swap_rope_halves(k_pe), zeros(LANE - MLA_QK),
        dt, zeros(LANE - SSD_HEADS)], axis=1)
    return (jnp.concatenate([z, xbc], axis=1).astype(BF16), uv.astype(BF16), gate.astype(BF16),
            small.astype(BF16))


def _pick(n, prefs):
    for p in prefs:
        if n % p == 0:
            return p
    return n


def kernel(x, mix_norm_g, w_in, conv_w, conv_b, dt_bias, a_log, d_skip, ssd_norm_g, sgu_ln_g, sgu_ln_b, sgu_w,
           sgu_b, q_norm_g, w_uq, kv_norm_g, w_ukv, q_head_g, k_head_g, w_br_ssd, w_br_sgu, w_br_mla, w_out,
           ffn_norm_g, ffn_w_gate, ffn_w_up, ffn_w_down, router_w, moe_w_gate, moe_w_up, moe_w_down):
    batch, seq, d = x.shape
    t = batch * seq
    depth = w_in.shape[0]
    tm = _pick(t, (1024, 512, 256, 128))
    xf = x.reshape(t, d)
    for i in range(depth):
        w_zx, w_uv, w_gate, w_small = _pack_in_proj(w_in[i])
        hn = rmsnorm(xf, mix_norm_g[i], tm)
        zx = matmul(hn, w_zx, BF16, tm, _pick(w_zx.shape[1], (1280,)), "in_proj_ssd")
        uv = matmul(hn, w_uv, BF16, tm, _pick(w_uv.shape[1], (1024,)), "in_proj_sgu")
        gate = matmul(hn, w_gate, BF16, tm, _pick(w_gate.shape[1], (1536,)), "in_proj_gate")
        small = matmul(hn, w_small, F32, tm, SMALL_WIDTH, "in_proj_small")
        y_ssd = ssd_branch(zx, small, conv_w[i], conv_b[i], dt_bias[i], a_log[i], d_skip[i], ssd_norm_g[i],
                           batch, seq, _pick(seq // SSD_CHUNK, (4, 2, 1)))
        y_sgu = sgu_branch(uv, sgu_ln_g[i], sgu_ln_b[i], sgu_w[i], sgu_b[i], _pick(seq // SGU_CHUNK, (4, 2, 1)))
        q, k, v = mla_prep(small, q_norm_g[i], w_uq[i], kv_norm_g[i], w_ukv[i], q_head_g[i], k_head_g[i],
                           batch, seq, _pick(seq, (512, 256, 128)))
        y_mla = flash_attention(q, k, v, batch, seq, _pick(seq, FLASH_TQ), _pick(seq, FLASH_TK)).reshape(t, MLA_WIDTH)
        is_moe = i % 2 == 1
        x1, h2 = merge(xf, y_ssd, y_sgu, y_mla, gate, w_br_ssd[i], w_br_sgu[i], w_br_mla[i], w_out[i],
                       ffn_norm_g[i], _pick(t, (512, 256, 128)), F32 if is_moe else BF16)
        j = i // 2
        if not is_moe:
            xf = ffn(h2, x1, ffn_w_gate[j], ffn_w_up[j], ffn_w_down[j], _pick(t, (512, 256, 128)),
                     _pick(ffn_w_gate.shape[2], (256, 128)))
        else:
            route = router(x1, ffn_norm_g[i], router_w[j], tm)
            tm_e = _pick(t, (512, 256, 128))
            tile_expert, tile_block, n_tiles, pos1, pos2 = _moe_plan(route, t, tm_e)
            xs = moe_dispatch(h2, pos1, pos2, (2 * t // tm_e + N_EXPERTS) * tm_e, tm_e)
            ys = moe_grouped(xs, tile_expert, tile_block, n_tiles, moe_w_gate[j], moe_w_up[j], moe_w_down[j],
                             tm_e, _pick(moe_w_gate.shape[3], (512, 256)))
            xf = moe_combine(x1, route, ys, pos1, pos2, tm_e)
    return xf.reshape(batch, seq, d)
```

```python
import functools
import math

import jax
import jax.numpy as jnp
from jax import lax
from jax.experimental import pallas as pl
from jax.experimental.pallas import tpu as pltpu

F32 = jnp.float32
BF16 = jnp.bfloat16

D_MODEL = 1024
SSD_HEADS = 16
SSD_HEAD_DIM = 64
SSD_INNER = SSD_HEADS * SSD_HEAD_DIM
SSD_GROUPS = 2
SSD_STATE = 128
SSD_CONV = 4
SSD_CHUNK = 128
SSD_CONV_CH = SSD_INNER + 2 * SSD_GROUPS * SSD_STATE
SGU_WIDTH = 1024
SGU_GROUPS = 8
SGU_CHUNK = 128
MLA_HEADS = 16
MLA_Q_RANK = 256
MLA_KV_RANK = 128
MLA_NOPE = 64
MLA_ROPE = 32
MLA_V = 64
MLA_QK = MLA_NOPE + MLA_ROPE
MLA_WIDTH = MLA_HEADS * MLA_V
ROPE_BASE = 10000.0
N_EXPERTS = 8
EPS = 1e-6
LN_EPS = 1e-5

LANE = 128
SUBLANE = 8
VMEM_LIMIT_BYTES = 48 * 1024 * 1024
MOE_VMEM_LIMIT_BYTES = 56 * 1024 * 1024

ROUTE_E1, ROUTE_E2, ROUTE_W1, ROUTE_W2 = 0, 1, 2, 3

SMALL_KPE_TILE = 3
SMALL_KPEP_TILE = 4
SMALL_DT_TILE = 5
SMALL_WIDTH = 6 * LANE


def _params(*semantics):
    return pltpu.CompilerParams(dimension_semantics=semantics, vmem_limit_bytes=VMEM_LIMIT_BYTES)


def _sigmoid(x):
    return 0.5 * jnp.tanh(0.5 * x) + 0.5


def _rmsnorm_kernel(x_ref, g_ref, o_ref):
    x = x_ref[...]
    ms = jnp.mean(x * x, axis=-1, keepdims=True)
    o_ref[...] = (x * lax.rsqrt(ms + EPS) * g_ref[...]).astype(o_ref.dtype)


def rmsnorm(x, g, tm):
    t, d = x.shape
    return pl.pallas_call(
        _rmsnorm_kernel,
        grid=(t // tm,),
        in_specs=[pl.BlockSpec((tm, d), lambda i: (i, 0)), pl.BlockSpec((1, d), lambda i: (0, 0))],
        out_specs=pl.BlockSpec((tm, d), lambda i: (i, 0)),
        out_shape=jax.ShapeDtypeStruct((t, d), BF16),
        compiler_params=_params("parallel"),
        name="rmsnorm",
    )(x, g.reshape(1, d))


def _matmul_kernel(a_ref, w_ref, o_ref):
    o_ref[...] = jnp.dot(a_ref[...], w_ref[...], preferred_element_type=F32).astype(o_ref.dtype)


def matmul(a, w, out_dtype, tm, tn, name):
    t, k = a.shape
    n = w.shape[1]
    return pl.pallas_call(
        _matmul_kernel,
        grid=(t // tm, n // tn),
        in_specs=[pl.BlockSpec((tm, k), lambda i, j: (i, 0)), pl.BlockSpec((k, tn), lambda i, j: (0, j))],
        out_specs=pl.BlockSpec((tm, tn), lambda i, j: (i, j)),
        out_shape=jax.ShapeDtypeStruct((t, n), out_dtype),
        compiler_params=_params("parallel", "parallel"),
        name=name,
    )(a, w)


def _ssd_kernel(zx_ref, dt_ref, convw_ref, convb_ref, dtb_ref, alog_ref, dskip_ref, ng_ref, exp_ref,
                o_ref, tail_ref, st_ref, *, chunks_per_step):
    l = SSD_CHUNK
    half = SSD_INNER // SSD_GROUPS

    @pl.when(pl.program_id(1) == 0)
    def _():
        tail_ref[...] = jnp.zeros_like(tail_ref)
        st_ref[...] = jnp.zeros_like(st_ref)

    row = lax.broadcasted_iota(jnp.int32, (l, l), 0)
    lane = lax.broadcasted_iota(jnp.int32, (l, l), 1)
    causal = row >= lane
    low_half = lane < SSD_HEAD_DIM
    tri = causal.astype(F32)
    a_neg = -jnp.exp(alog_ref[...])
    shifts = [(lane == row - (SSD_CONV - 1 - j)).astype(BF16) for j in range(SSD_CONV - 1)]

    for ci in range(chunks_per_step):
        r0 = ci * l
        z = zx_ref[r0:r0 + l, 0:SSD_INNER].astype(F32)
        cur = zx_ref[r0:r0 + l, SSD_INNER:SSD_INNER + SSD_CONV_CH]
        conv = convb_ref[...] + convw_ref[SSD_CONV - 1:SSD_CONV, :] * cur.astype(F32)
        head = jnp.zeros((SUBLANE, SSD_CONV_CH), F32)
        for j in range(SSD_CONV - 1):
            d = SSD_CONV - 1 - j
            conv = conv + convw_ref[j:j + 1, :] * jnp.dot(shifts[j], cur, preferred_element_type=F32)
            head = head + convw_ref[j:j + 1, :] * tail_ref[SUBLANE - d:2 * SUBLANE - d, :]
        conv = jnp.concatenate([conv[0:SUBLANE] + head, conv[SUBLANE:]], axis=0)
        tail_ref[0:SUBLANE, :] = cur[l - SUBLANE:l, :].astype(F32)
        xbc = conv * _sigmoid(conv)
        xh = xbc[:, 0:SSD_INNER]
        bm = xbc[:, SSD_INNER:SSD_INNER + SSD_GROUPS * SSD_STATE].astype(BF16)
        cm = xbc[:, SSD_INNER + SSD_GROUPS * SSD_STATE:].astype(BF16)

        dt_in = dt_ref[r0:r0 + l, :] + dtb_ref[...]
        dt = jnp.maximum(dt_in, 0.0) + jnp.log1p(jnp.exp(-jnp.abs(dt_in)))
        adt = dt * a_neg
        cs = jnp.dot(tri, adt, preferred_element_type=F32, precision=lax.Precision.HIGHEST)
        cs_t = cs.T
        cs_last = cs[l - 1:l, :]
        small = jnp.concatenate([dt, jnp.exp(cs_last - cs), jnp.exp(cs)], axis=0).astype(BF16)
        wide = jnp.dot(small, exp_ref[...], preferred_element_type=F32)
        dt_e = wide[0:l]
        dstate_e = wide[l:2 * l]
        ecs_e = wide[2 * l:3 * l]
        cdec_e = ecs_e[l - 1:l, :]

        xs = xh * dt_e
        xs_b = xs.astype(BF16)
        xd_b = (xs * dstate_e).astype(BF16)

        y_parts = []
        off_parts = []
        for g in range(SSD_GROUPS):
            bg = bm[:, g * SSD_STATE:(g + 1) * SSD_STATE]
            cg = cm[:, g * SSD_STATE:(g + 1) * SSD_STATE]
            cb = lax.dot_general(cg, bg, (((1,), (1,)), ((), ())), preferred_element_type=F32)
            for hp in range(SSD_HEADS // SSD_GROUPS // 2):
                pair = g * (SSD_HEADS // SSD_GROUPS // 2) + hp
                xs_pair = xs_b[:, pair * LANE:(pair + 1) * LANE]
                ys = []
                for h in (2 * pair, 2 * pair + 1):
                    diff = cs[:, h:h + 1] - cs_t[h:h + 1, :]
                    decay = jnp.exp(jnp.where(causal, diff, -jnp.inf))
                    ys.append(jnp.dot((cb * decay).astype(BF16), xs_pair, preferred_element_type=F32))
                y_parts.append(jnp.where(low_half, ys[0], ys[1]))
            st_g = st_ref[:, g * half:(g + 1) * half]
            off_parts.append(jnp.dot(cg, st_g.astype(BF16), preferred_element_type=F32))
            new = lax.dot_general(bg, xd_b[:, g * half:(g + 1) * half], (((0,), (0,)), ((), ())),
                                  preferred_element_type=F32)
            st_ref[:, g * half:(g + 1) * half] = st_g * cdec_e[:, g * half:(g + 1) * half] + new

        y = (jnp.concatenate(y_parts, axis=1) + jnp.concatenate(off_parts, axis=1) * ecs_e
             + xh * dskip_ref[...])
        yg = y * (z * _sigmoid(z))
        outs = []
        for g in range(SSD_GROUPS):
            yh = yg[:, g * half:(g + 1) * half]
            ms = jnp.mean(yh * yh, axis=-1, keepdims=True)
            outs.append(yh * lax.rsqrt(ms + EPS))
        o_ref[r0:r0 + l, :] = (jnp.concatenate(outs, axis=1) * ng_ref[...]).astype(o_ref.dtype)


def ssd_branch(zx, small, conv_w, conv_b, dt_bias, a_log, d_skip, norm_g, batch, seq, chunks_per_step):
    rows = chunks_per_step * SSD_CHUNK
    steps = seq // rows
    pad_h = LANE - SSD_HEADS
    dtb = jnp.pad(dt_bias, (0, pad_h)).reshape(1, LANE)
    alog = jnp.pad(a_log, (0, pad_h)).reshape(1, LANE)
    dskip = jnp.repeat(d_skip, SSD_HEAD_DIM).reshape(1, SSD_INNER)
    expand = (jnp.arange(LANE)[:, None] == jnp.arange(SSD_INNER)[None, :] // SSD_HEAD_DIM).astype(BF16)
    const = lambda b, c: (0, 0)
    return pl.pallas_call(
        functools.partial(_ssd_kernel, chunks_per_step=chunks_per_step),
        grid=(batch, steps),
        in_specs=[
            pl.BlockSpec((rows, SSD_INNER + SSD_CONV_CH), lambda b, c: (b * steps + c, 0)),
            pl.BlockSpec((rows, LANE), lambda b, c: (b * steps + c, SMALL_DT_TILE)),
            pl.BlockSpec((SSD_CONV, SSD_CONV_CH), const),
            pl.BlockSpec((1, SSD_CONV_CH), const),
            pl.BlockSpec((1, LANE), const),
            pl.BlockSpec((1, LANE), const),
            pl.BlockSpec((1, SSD_INNER), const),
            pl.BlockSpec((1, SSD_INNER), const),
            pl.BlockSpec((LANE, SSD_INNER), const),
        ],
        out_specs=pl.BlockSpec((rows, SSD_INNER), lambda b, c: (b * steps + c, 0)),
        out_shape=jax.ShapeDtypeStruct((batch * seq, SSD_INNER), BF16),
        scratch_shapes=[
            pltpu.VMEM((2 * SUBLANE, SSD_CONV_CH), F32),
            pltpu.VMEM((SSD_STATE, SSD_INNER), F32),
        ],
        compiler_params=_params("parallel", "arbitrary"),
        name="ssd_branch",
    )(zx, small, conv_w, conv_b.reshape(1, -1), dtb, alog, dskip, norm_g.reshape(1, -1), expand)


def _sgu_kernel(uv_ref, g_ref, b_ref, w_ref, bias_ref, o_ref, *, chunks_per_step):
    l = SGU_CHUNK
    uv = uv_ref[...].astype(F32)
    act = 0.5 * uv * (1.0 + lax.erf(uv * (1.0 / math.sqrt(2.0))))
    u = act[:, 0:SGU_WIDTH]
    v = act[:, SGU_WIDTH:]
    mu = jnp.mean(v, axis=-1, keepdims=True)
    vc = v - mu
    var = jnp.mean(vc * vc, axis=-1, keepdims=True)
    vn = (vc * lax.rsqrt(var + LN_EPS) * g_ref[...] + b_ref[...]).astype(BF16)
    row = lax.broadcasted_iota(jnp.int32, (l, l), 0)
    col = lax.broadcasted_iota(jnp.int32, (l, l), 1)
    causal = row >= col
    for g in range(SGU_GROUPS):
        wg = jnp.where(causal, w_ref[g], 0.0).astype(BF16)
        c0 = g * LANE
        for ci in range(chunks_per_step):
            r0 = ci * l
            mixed = jnp.dot(wg, vn[r0:r0 + l, c0:c0 + LANE], preferred_element_type=F32)
            o_ref[r0:r0 + l, c0:c0 + LANE] = (
                u[r0:r0 + l, c0:c0 + LANE] * (mixed + bias_ref[:, c0:c0 + LANE])).astype(o_ref.dtype)


def sgu_branch(uv, ln_g, ln_b, w_s, b_s, chunks_per_step):
    t = uv.shape[0]
    rows = chunks_per_step * SGU_CHUNK
    bias = jnp.repeat(b_s.T, SGU_WIDTH // SGU_GROUPS, axis=1)
    const2 = lambda i: (0, 0)
    return pl.pallas_call(
        functools.partial(_sgu_kernel, chunks_per_step=chunks_per_step),
        grid=(t // rows,),
        in_specs=[
            pl.BlockSpec((rows, 2 * SGU_WIDTH), lambda i: (i, 0)),
            pl.BlockSpec((1, SGU_WIDTH), const2),
            pl.BlockSpec((1, SGU_WIDTH), const2),
            pl.BlockSpec((SGU_GROUPS, SGU_CHUNK, SGU_CHUNK), lambda i: (0, 0, 0)),
            pl.BlockSpec((SGU_CHUNK, SGU_WIDTH), const2),
        ],
        out_specs=pl.BlockSpec((rows, SGU_WIDTH), lambda i: (i, 0)),
        out_shape=jax.ShapeDtypeStruct((t, SGU_WIDTH), BF16),
        compiler_params=_params("parallel"),
        name="sgu_branch",
    )(uv, ln_g.reshape(1, -1), ln_b.reshape(1, -1), w_s, bias)


def _mla_prep_kernel(small_ref, cos_ref, sin_ref, qng_ref, kvng_ref, qhg_ref, khg_ref, wq_ref, wqp_ref, wk_ref,
                     wv_ref, q_ref, k_ref, v_ref):
    sm = small_ref[...]
    q_lat = sm[:, 0:MLA_Q_RANK]
    kv_lat = sm[:, MLA_Q_RANK:MLA_Q_RANK + MLA_KV_RANK]
    kpe = sm[:, SMALL_KPE_TILE * LANE:(SMALL_KPE_TILE + 1) * LANE]
    kpe_p = sm[:, SMALL_KPEP_TILE * LANE:(SMALL_KPEP_TILE + 1) * LANE]

    def rms(x, g):
        return x * lax.rsqrt(jnp.mean(x * x, axis=-1, keepdims=True) + EPS) * g

    ql = rms(q_lat, qng_ref[...]).astype(BF16)
    kvl = rms(kv_lat, kvng_ref[...]).astype(BF16)
    v_ref[...] = jnp.dot(kvl, wv_ref[...], preferred_element_type=F32).astype(v_ref.dtype)

    pair_w = 2 * LANE
    same_head = ((lax.broadcasted_iota(jnp.int32, (pair_w, pair_w), 0) < LANE)
                 == (lax.broadcasted_iota(jnp.int32, (pair_w, pair_w), 1) < LANE))
    ones_pair = same_head.astype(BF16)

    def head_inv_rms(x):
        sq = (x * x).astype(BF16)
        ssq = jnp.concatenate(
            [jnp.dot(sq[:, p * pair_w:(p + 1) * pair_w], ones_pair, preferred_element_type=F32)
             for p in range(MLA_HEADS // 2)], axis=1)
        return lax.rsqrt(ssq * (1.0 / MLA_QK) + EPS)

    cos = cos_ref[...]
    sin = sin_ref[...]
    rep = lambda a: jnp.tile(a, (1, MLA_HEADS))

    q_scale = MLA_QK ** -0.5 * math.log2(math.e)
    q_x = jnp.dot(ql, wq_ref[...], preferred_element_type=F32)
    q_xp = jnp.dot(ql, wqp_ref[...], preferred_element_type=F32)
    q_a = cos * qhg_ref[0:1, :] * q_scale
    q_b = sin * qhg_ref[1:2, :] * q_scale
    q_out = head_inv_rms(q_x) * (q_x * rep(q_a) + q_xp * rep(q_b))
    k_x = jnp.dot(kvl, wk_ref[...], preferred_element_type=F32) + rep(kpe)
    k_a = cos * khg_ref[0:1, :]
    k_b = sin * khg_ref[1:2, :]
    k_out = head_inv_rms(k_x) * (k_x * rep(k_a) + rep(kpe_p * k_b))
    for h in range(MLA_HEADS):
        q_ref[0, h] = q_out[:, h * LANE:(h + 1) * LANE].astype(q_ref.dtype)
        k_ref[0, h] = k_out[:, h * LANE:(h + 1) * LANE].astype(k_ref.dtype)


def _swap_rope_halves(a):
    half = MLA_ROPE // 2
    return jnp.concatenate([a[..., half:], a[..., :half]], axis=-1)


def _head_gain_rows(g):
    direct = jnp.pad(g, (0, LANE - MLA_QK))
    partner = jnp.pad(_swap_rope_halves(g[MLA_NOPE:]), (MLA_NOPE, LANE - MLA_QK))
    return jnp.stack([direct, partner])


def mla_prep(small, q_norm_g, w_uq, kv_norm_g, w_ukv, q_head_g, k_head_g, batch, seq, rows):
    steps = seq // rows
    w_uq3 = w_uq.reshape(MLA_Q_RANK, MLA_HEADS, MLA_QK)
    wq = jnp.pad(w_uq3, ((0, 0), (0, 0), (0, LANE - MLA_QK))).reshape(MLA_Q_RANK, -1).astype(BF16)
    wqp = jnp.pad(_swap_rope_halves(w_uq3[:, :, MLA_NOPE:]), ((0, 0), (0, 0), (MLA_NOPE, LANE - MLA_QK)))
    wqp = wqp.reshape(MLA_Q_RANK, -1).astype(BF16)
    w_ukv3 = w_ukv.reshape(MLA_KV_RANK, MLA_HEADS, MLA_NOPE + MLA_V)
    wk = jnp.pad(w_ukv3[:, :, :MLA_NOPE], ((0, 0), (0, 0), (0, LANE - MLA_NOPE))).reshape(MLA_KV_RANK, -1).astype(BF16)
    wv = w_ukv3[:, :, MLA_NOPE:].reshape(MLA_KV_RANK, MLA_WIDTH).astype(BF16)
    qhg = _head_gain_rows(q_head_g)
    khg = _head_gain_rows(k_head_g)
    inv = 1.0 / (ROPE_BASE ** (jnp.arange(0, MLA_ROPE, 2, dtype=F32) / MLA_ROPE))
    ang = jnp.arange(seq, dtype=F32)[:, None] * inv[None, :]
    c, s = jnp.cos(ang), jnp.sin(ang)
    ones = jnp.ones((seq, MLA_NOPE), F32)
    zeros = jnp.zeros((seq, MLA_NOPE), F32)
    tail1 = jnp.ones((seq, LANE - MLA_QK), F32)
    tail0 = jnp.zeros((seq, LANE - MLA_QK), F32)
    cos_t = jnp.concatenate([ones, c, c, tail1], axis=1)
    sin_t = jnp.concatenate([zeros, -s, s, tail0], axis=1)
    const = lambda b, i: (0, 0)
    qk_shape = jax.ShapeDtypeStruct((batch, MLA_HEADS, seq, LANE), BF16)
    return pl.pallas_call(
        _mla_prep_kernel,
        grid=(batch, steps),
        in_specs=[
            pl.BlockSpec((rows, SMALL_WIDTH), lambda b, i: (b * steps + i, 0)),
            pl.BlockSpec((rows, LANE), lambda b, i: (i, 0)),
            pl.BlockSpec((rows, LANE), lambda b, i: (i, 0)),
            pl.BlockSpec((1, MLA_Q_RANK), const),
            pl.BlockSpec((1, MLA_KV_RANK), const),
            pl.BlockSpec((2, LANE), const),
            pl.BlockSpec((2, LANE), const),
            pl.BlockSpec((MLA_Q_RANK, MLA_HEADS * LANE), const),
            pl.BlockSpec((MLA_Q_RANK, MLA_HEADS * LANE), const),
            pl.BlockSpec((MLA_KV_RANK, MLA_HEADS * LANE), const),
            pl.BlockSpec((MLA_KV_RANK, MLA_WIDTH), const),
        ],
        out_specs=[
            pl.BlockSpec((1, MLA_HEADS, rows, LANE), lambda b, i: (b, 0, i, 0)),
            pl.BlockSpec((1, MLA_HEADS, rows, LANE), lambda b, i: (b, 0, i, 0)),
            pl.BlockSpec((rows, MLA_WIDTH), lambda b, i: (b * steps + i, 0)),
        ],
        out_shape=[qk_shape, qk_shape, jax.ShapeDtypeStruct((batch * seq, MLA_WIDTH), BF16)],
        compiler_params=_params("parallel", "parallel"),
        name="mla_prep",
    )(small, cos_t, sin_t, q_norm_g.reshape(1, -1), kv_norm_g.reshape(1, -1), qhg, khg, wq, wqp, wk, wv)


FLASH_HEADS_PER_STEP = 8
FLASH_TQ = (1024, 512, 256, 128)
FLASH_TK = (512, 256, 128)


def _flash_kernel(qi_ref, kj_ref, q_ref, k_ref, v_ref, o_ref, m_ref, acc_ref, *, tq, tk):
    t = pl.program_id(2)
    qi = qi_ref[t]
    kj = kj_ref[t]
    heads = FLASH_HEADS_PER_STEP
    band = kj * tk - qi * tq

    @pl.when((kj == 0) & (band >= 0))
    def _():
        m_ref[...] = jnp.full_like(m_ref, -jnp.inf)
        acc_ref[...] = jnp.zeros_like(acc_ref)

    def step(lo, hi, triangular, first_block=False):
        rows = hi - lo
        first = lax.broadcasted_iota(jnp.int32, (tk, LANE), 1) < MLA_V
        one = jnp.ones((tk, LANE), BF16)
        if triangular:
            visible = (lax.broadcasted_iota(jnp.int32, (rows, tk), 1)
                       <= lax.broadcasted_iota(jnp.int32, (rows, tk), 0))
        for hh in range(heads):
            v = v_ref[0, :, (hh // 2) * LANE:(hh // 2 + 1) * LANE]
            v_ones = jnp.where(first, v, one) if hh % 2 == 0 else jnp.where(first, one, v)
            s = lax.dot_general(q_ref[0, hh, lo:hi, :], k_ref[0, hh], (((1,), (1,)), ((), ())),
                                preferred_element_type=F32)
            if triangular:
                s = jnp.where(visible, s, -jnp.inf)
            row_max = jnp.max(s, axis=1, keepdims=True)
            if first_block:
                m_new = jnp.broadcast_to(row_max, (rows, LANE))
                p = jnp.exp2(s - jnp.tile(m_new, (1, tk // LANE)))
                acc_ref[hh, lo:hi, :] = jnp.dot(p.astype(BF16), v_ones, preferred_element_type=F32)
            else:
                m_prev = m_ref[hh, lo:hi, :]
                m_new = jnp.maximum(m_prev, row_max)
                p = jnp.exp2(s - jnp.tile(m_new, (1, tk // LANE)))
                alpha = jnp.exp2(m_prev - m_new)
                acc_ref[hh, lo:hi, :] = (alpha * acc_ref[hh, lo:hi, :]
                                         + jnp.dot(p.astype(BF16), v_ones, preferred_element_type=F32))
            m_ref[hh, lo:hi, :] = m_new

    def finalize():
        lane = lax.broadcasted_iota(jnp.int32, (tq, LANE), 1)
        for pair in range(heads // 2):
            a0 = acc_ref[2 * pair]
            a1 = acc_ref[2 * pair + 1]
            out = jnp.where(lane < MLA_V, a0 / pltpu.roll(a0, MLA_V, 1), a1 / pltpu.roll(a1, MLA_V, 1))
            o_ref[0, :, pair * LANE:(pair + 1) * LANE] = out.astype(o_ref.dtype)

    @pl.when((band < 0) & (kj == 0))
    def _():
        step(0, tq, False, first_block=True)

    @pl.when((band < 0) & (kj > 0))
    def _():
        step(0, tq, False)

    n_band = tq // tk
    for j in range(n_band):
        @pl.when(band == j * tk)
        def _(j=j):
            step(j * tk, (j + 1) * tk, True)
            if j + 1 < n_band:
                step((j + 1) * tk, tq, False)
            else:
                finalize()


def flash_attention(q, k, v, batch, seq, tq, tk):
    assert tq % tk == 0 and seq % tq == 0
    heads = FLASH_HEADS_PER_STEP
    width = heads * MLA_V
    pairs = [(i, j) for i in range(seq // tq) for j in range((i + 1) * tq // tk)]
    qi = jnp.asarray([p[0] for p in pairs], jnp.int32)
    kj = jnp.asarray([p[1] for p in pairs], jnp.int32)
    grid_spec = pltpu.PrefetchScalarGridSpec(
        num_scalar_prefetch=2,
        grid=(batch, MLA_HEADS // heads, len(pairs)),
        in_specs=[
            pl.BlockSpec((1, heads, tq, LANE), lambda b, h, t, qi, kj: (b, h, qi[t], 0)),
            pl.BlockSpec((1, heads, tk, LANE), lambda b, h, t, qi, kj: (b, h, kj[t], 0)),
            pl.BlockSpec((1, tk, width), lambda b, h, t, qi, kj: (b, kj[t], h)),
        ],
        out_specs=pl.BlockSpec((1, tq, width), lambda b, h, t, qi, kj: (b, qi[t], h)),
        scratch_shapes=[
            pltpu.VMEM((heads, tq, LANE), F32),
            pltpu.VMEM((heads, tq, LANE), F32),
        ],
    )
    return pl.pallas_call(
        functools.partial(_flash_kernel, tq=tq, tk=tk),
        grid_spec=grid_spec,
        out_shape=jax.ShapeDtypeStruct((batch, seq, MLA_WIDTH), BF16),
        compiler_params=_params("parallel", "parallel", "arbitrary"),
        name="mla_flash_attention",
    )(qi, kj, q, k, v.reshape(batch, seq, MLA_WIDTH))


def _merge_kernel(x_ref, ys_ref, yg_ref, ym_ref, gate_ref, ws_ref, wg_ref, wm_ref, wo_ref, g2_ref,
                  x1_ref, h2_ref):
    d = D_MODEL
    gate = _sigmoid(gate_ref[...].astype(F32))
    merged = (gate[:, 0:d] * jnp.dot(ys_ref[...], ws_ref[...], preferred_element_type=F32)
              + gate[:, d:2 * d] * jnp.dot(yg_ref[...], wg_ref[...], preferred_element_type=F32)
              + gate[:, 2 * d:] * jnp.dot(ym_ref[...], wm_ref[...], preferred_element_type=F32))
    x1 = x_ref[...] + jnp.dot(merged.astype(BF16), wo_ref[...], preferred_element_type=F32)
    x1_ref[...] = x1
    ms = jnp.mean(x1 * x1, axis=-1, keepdims=True)
    h2_ref[...] = (x1 * lax.rsqrt(ms + EPS) * g2_ref[...]).astype(h2_ref.dtype)


def merge(x, y_ssd, y_sgu, y_mla, gate, w_ssd, w_sgu, w_mla, w_out, ffn_norm_g, tm, h2_dtype):
    t, d = x.shape
    row = lambda i: (i, 0)
    const = lambda i: (0, 0)
    act = pl.BlockSpec((tm, d), row)
    wspec = pl.BlockSpec((d, d), const)
    return pl.pallas_call(
        _merge_kernel,
        grid=(t // tm,),
        in_specs=[act, act, act, act, pl.BlockSpec((tm, 3 * d), row), wspec, wspec, wspec, wspec,
                  pl.BlockSpec((1, d), const)],
        out_specs=[act, act],
        out_shape=[jax.ShapeDtypeStruct((t, d), F32), jax.ShapeDtypeStruct((t, d), h2_dtype)],
        compiler_params=_params("parallel"),
        name="merge_out_proj",
    )(x, y_ssd, y_sgu, y_mla, gate, w_ssd.astype(BF16), w_sgu.astype(BF16), w_mla.astype(BF16),
      w_out.astype(BF16), ffn_norm_g.reshape(1, d))


def _ffn_kernel(h_ref, x_ref, wg_ref, wu_ref, wd_ref, o_ref, *, tf):
    h = h_ref[...]
    acc = x_ref[...]
    for c in range(wg_ref.shape[1] // tf):
        a = jnp.dot(h, wg_ref[:, c * tf:(c + 1) * tf], preferred_element_type=F32)
        b = jnp.dot(h, wu_ref[:, c * tf:(c + 1) * tf], preferred_element_type=F32)
        act = ((a * _sigmoid(a)) * b).astype(BF16)
        acc = acc + jnp.dot(act, wd_ref[c * tf:(c + 1) * tf, :], preferred_element_type=F32)
    o_ref[...] = acc


def ffn(h, x, wg, wu, wd, tm, tf):
    t, d = x.shape
    f_dim = wg.shape[1]
    row = lambda i: (i, 0)
    const = lambda i: (0, 0)
    return pl.pallas_call(
        functools.partial(_ffn_kernel, tf=tf),
        grid=(t // tm,),
        in_specs=[
            pl.BlockSpec((tm, d), row),
            pl.BlockSpec((tm, d), row),
            pl.BlockSpec((d, f_dim), const),
            pl.BlockSpec((d, f_dim), const),
            pl.BlockSpec((f_dim, d), const),
        ],
        out_specs=pl.BlockSpec((tm, d), row),
        out_shape=jax.ShapeDtypeStruct((t, d), F32),
        compiler_params=_params("parallel"),
        name="swiglu_ffn",
    )(h, x, wg.astype(BF16), wu.astype(BF16), wd.astype(BF16))


def _router_kernel(x_ref, g_ref, rw_ref, c_ref):
    x = x_ref[...]
    h = x * lax.rsqrt(jnp.mean(x * x, axis=-1, keepdims=True) + EPS) * g_ref[...]
    h_hi = h.astype(BF16)
    h_lo = (h - h_hi.astype(F32)).astype(BF16)
    w = rw_ref[...]
    w_hi = w.astype(BF16)
    w_lo = (w - w_hi.astype(F32)).astype(BF16)
    logits = (jnp.dot(h_hi, w_hi, preferred_element_type=F32) + jnp.dot(h_lo, w_hi, preferred_element_type=F32)
              + jnp.dot(h_hi, w_lo, preferred_element_type=F32))
    lane = lax.broadcasted_iota(jnp.int32, logits.shape, 1)
    lg = jnp.where(lane < N_EXPERTS, logits, -jnp.inf)
    m1 = jnp.max(lg, axis=-1, keepdims=True)
    i1 = jnp.min(jnp.where(lg == m1, lane, LANE), axis=-1, keepdims=True)
    lg2 = jnp.where(lane == i1, -jnp.inf, lg)
    m2 = jnp.max(lg2, axis=-1, keepdims=True)
    i2 = jnp.min(jnp.where(lg2 == m2, lane, LANE), axis=-1, keepdims=True)
    e2 = jnp.exp(m2 - m1)
    w1 = 1.0 / (1.0 + e2)
    w2 = e2 / (1.0 + e2)
    c_ref[...] = (jnp.where(lane == ROUTE_E1, i1.astype(F32), 0.0) + jnp.where(lane == ROUTE_E2, i2.astype(F32), 0.0)
                  + jnp.where(lane == ROUTE_W1, w1, 0.0) + jnp.where(lane == ROUTE_W2, w2, 0.0))


def router(x, g, router_w, tm):
    t, d = x.shape
    rw = jnp.pad(router_w, ((0, 0), (0, LANE - N_EXPERTS)))
    return pl.pallas_call(
        _router_kernel,
        grid=(t // tm,),
        in_specs=[pl.BlockSpec((tm, d), lambda i: (i, 0)), pl.BlockSpec((1, d), lambda i: (0, 0)),
                  pl.BlockSpec((d, LANE), lambda i: (0, 0))],
        out_specs=pl.BlockSpec((tm, LANE), lambda i: (i, 0)),
        out_shape=jax.ShapeDtypeStruct((t, LANE), F32),
        compiler_params=_params("parallel"),
        name="moe_router",
    )(x, g.reshape(1, d), rw)


def _moe_plan(route, t, tm):
    n_assign = 2 * t
    n_tiles_max = n_assign // tm + N_EXPERTS
    e = jnp.concatenate([route[:, ROUTE_E1], route[:, ROUTE_E2]]).astype(jnp.int32)
    onehot = (e[:, None] == jnp.arange(N_EXPERTS, dtype=jnp.int32)[None, :]).astype(jnp.int32)
    csum = jnp.cumsum(onehot, axis=0)
    rank = jnp.sum(csum * onehot, axis=1) - 1
    counts = csum[-1]
    tiles_per = (counts + tm - 1) // tm
    tile_end = jnp.cumsum(tiles_per)
    n_tiles = tile_end[-1]
    pos = jnp.sum(onehot * (tile_end - tiles_per)[None, :], axis=1) * tm + rank
    tile = jnp.minimum(jnp.arange(n_tiles_max, dtype=jnp.int32), n_tiles - 1)
    tile_expert = jnp.sum((tile[:, None] >= tile_end[None, :]).astype(jnp.int32), axis=1)
    shape3 = (t // tm, 1, tm)
    return (tile_expert, tile.astype(jnp.int32), n_tiles.reshape(1).astype(jnp.int32),
            pos[:t].reshape(shape3), pos[t:].reshape(shape3))


def _moe_dispatch_kernel(pos1_ref, pos2_ref, h_ref, init_hbm, xs_hbm, stage, sem, *, tm):
    del init_hbm
    i = pl.program_id(0)
    last = pl.num_programs(0) - 1

    def row_copy(r, slot, parity):
        return pltpu.make_async_copy(stage.at[parity, pl.ds(r, 1), :], xs_hbm.at[pl.ds(slot, 1), :], sem.at[parity])

    def drain(parity):
        for _ in range(2):
            pltpu.make_async_copy(stage.at[parity], xs_hbm.at[pl.ds(0, tm), :], sem.at[parity]).wait()

    for parity in range(2):
        @pl.when(i % 2 == parity)
        def _(parity=parity):
            stage[parity] = h_ref[...]
            for r in range(tm):
                row_copy(r, pos1_ref[0, 0, r], parity).start(priority=0)
                row_copy(r, pos2_ref[0, 0, r], parity).start(priority=1)

            @pl.when(i >= 1)
            def _():
                drain(1 - parity)

            @pl.when(i == last)
            def _():
                drain(parity)


def moe_dispatch(h, pos1, pos2, n_slots, tm):
    t, d = h.shape
    smem_tile = pl.BlockSpec((1, 1, tm), lambda i: (i, 0, 0), memory_space=pltpu.SMEM)
    return pl.pallas_call(
        functools.partial(_moe_dispatch_kernel, tm=tm),
        grid=(t // tm,),
        in_specs=[smem_tile, smem_tile, pl.BlockSpec((tm, d), lambda i: (i, 0)), pl.BlockSpec(memory_space=pl.ANY)],
        out_specs=pl.BlockSpec(memory_space=pl.ANY),
        out_shape=jax.ShapeDtypeStruct((n_slots, d), h.dtype),
        scratch_shapes=[pltpu.VMEM((2, tm, d), h.dtype), pltpu.SemaphoreType.DMA((2,))],
        input_output_aliases={3: 0},
        compiler_params=_params("arbitrary"),
        name="moe_dispatch",
    )(pos1, pos2, h, jnp.zeros((n_slots, d), h.dtype))


def _moe_group_kernel(texp_ref, tile_ref, ntiles_ref, x_ref, wg_hbm, wu_hbm, wd_hbm, y_ref,
                      wg_s, wu_s, wd_s, stage_in, stage_out, sem, *, tf):
    del tile_ref
    i = pl.program_id(0)
    f_dim = wg_s.shape[1]
    n_chunk = f_dim // tf
    used = i < ntiles_ref[0]
    expert = texp_ref[i]
    new_expert = (i == 0) | (expert != texp_ref[jnp.maximum(i - 1, 0)])

    @pl.when(jnp.logical_not(used))
    def _():
        y_ref[...] = jnp.zeros_like(y_ref)

    def chunk_copy(k):
        c = k % n_chunk
        slot = k % 2
        if k < 2 * n_chunk:
            src = (wg_hbm if k < n_chunk else wu_hbm).at[expert, :, pl.ds(c * tf, tf)]
            return pltpu.make_async_copy(src, stage_in.at[slot], sem.at[slot])
        return pltpu.make_async_copy(wd_hbm.at[expert, pl.ds(c * tf, tf), :], stage_out.at[slot], sem.at[slot])

    @pl.when(used & new_expert)
    def _():
        chunk_copy(0).start()
        for k in range(3 * n_chunk):
            if k + 1 < 3 * n_chunk:
                chunk_copy(k + 1).start()
            chunk_copy(k).wait()
            c = k % n_chunk
            if k < n_chunk:
                wg_s[:, c * tf:(c + 1) * tf] = stage_in[k % 2].astype(BF16)
            elif k < 2 * n_chunk:
                wu_s[:, c * tf:(c + 1) * tf] = stage_in[k % 2].astype(BF16)
            else:
                wd_s[c * tf:(c + 1) * tf, :] = stage_out[k % 2].astype(BF16)

    @pl.when(used)
    def _():
        h = x_ref[...].astype(BF16)
        acc = jnp.zeros(y_ref.shape, F32)
        for c in range(n_chunk):
            a = jnp.dot(h, wg_s[:, c * tf:(c + 1) * tf], preferred_element_type=F32)
            b = jnp.dot(h, wu_s[:, c * tf:(c + 1) * tf], preferred_element_type=F32)
            act = ((a * _sigmoid(a)) * b).astype(BF16)
            acc = acc + jnp.dot(act, wd_s[c * tf:(c + 1) * tf, :], preferred_element_type=F32)
        y_ref[...] = acc


def moe_grouped(xs, tile_expert, tile_block, n_tiles, wg, wu, wd, tm, tf):
    n_slots, d = xs.shape
    f_dim = wg.shape[2]
    row_spec = pl.BlockSpec((tm, d), lambda i, te, tb, nt: (tb[i], 0))
    hbm_spec = pl.BlockSpec(memory_space=pl.ANY)
    grid_spec = pltpu.PrefetchScalarGridSpec(
        num_scalar_prefetch=3,
        grid=(n_slots // tm,),
        in_specs=[row_spec, hbm_spec, hbm_spec, hbm_spec],
        out_specs=pl.BlockSpec((tm, d), lambda i, te, tb, nt: (i, 0)),
        scratch_shapes=[
            pltpu.VMEM((d, f_dim), BF16), pltpu.VMEM((d, f_dim), BF16), pltpu.VMEM((f_dim, d), BF16),
            pltpu.VMEM((2, d, tf), wg.dtype), pltpu.VMEM((2, tf, d), wd.dtype),
            pltpu.SemaphoreType.DMA((2,)),
        ],
    )
    return pl.pallas_call(
        functools.partial(_moe_group_kernel, tf=tf),
        grid_spec=grid_spec,
        out_shape=jax.ShapeDtypeStruct((n_slots, d), F32),
        compiler_params=pltpu.CompilerParams(dimension_semantics=("arbitrary",),
                                             vmem_limit_bytes=MOE_VMEM_LIMIT_BYTES),
        name="moe_grouped_swiglu",
    )(tile_expert, tile_block, n_tiles, xs, wg, wu, wd)


def _moe_combine_kernel(pos1_ref, pos2_ref, pos1_next_ref, pos2_next_ref, x_ref, r_ref, y_hbm, o_ref, gbuf, sem,
                        *, tm):
    i = pl.program_id(0)
    last = pl.num_programs(0) - 1
    cur = i % 2

    def start_gather(p1_ref, p2_ref, buf):
        for r in range(tm):
            pltpu.make_async_copy(y_hbm.at[pl.ds(p1_ref[0, 0, r], 1), :], gbuf.at[buf, pl.ds(r, 1), :],
                                  sem.at[buf]).start(priority=0)
            pltpu.make_async_copy(y_hbm.at[pl.ds(p2_ref[0, 0, r], 1), :], gbuf.at[buf, pl.ds(tm + r, 1), :],
                                  sem.at[buf]).start(priority=1)

    @pl.when(i == 0)
    def _():
        start_gather(pos1_ref, pos2_ref, 0)

    for parity in range(2):
        @pl.when((i < last) & (cur == parity))
        def _(parity=parity):
            start_gather(pos1_next_ref, pos2_next_ref, 1 - parity)

    pltpu.make_async_copy(y_hbm.at[pl.ds(0, 2 * tm), :], gbuf.at[cur], sem.at[cur]).wait()
    r = r_ref[...]
    o_ref[...] = (x_ref[...] + r[:, ROUTE_W1:ROUTE_W1 + 1] * gbuf[cur, 0:tm, :]
                  + r[:, ROUTE_W2:ROUTE_W2 + 1] * gbuf[cur, tm:2 * tm, :])


def moe_combine(x, route, y, pos1, pos2, tm):
    t, d = x.shape
    nb = t // tm
    smem_cur = pl.BlockSpec((1, 1, tm), lambda i: (i, 0, 0), memory_space=pltpu.SMEM)
    smem_next = pl.BlockSpec((1, 1, tm), lambda i: (jnp.minimum(i + 1, nb - 1), 0, 0), memory_space=pltpu.SMEM)
    return pl.pallas_call(
        functools.partial(_moe_combine_kernel, tm=tm),
        grid=(nb,),
        in_specs=[smem_cur, smem_cur, smem_next, smem_next,
                  pl.BlockSpec((tm, d), lambda i: (i, 0)), pl.BlockSpec((tm, LANE), lambda i: (i, 0)),
                  pl.BlockSpec(memory_space=pl.ANY)],
        out_specs=pl.BlockSpec((tm, d), lambda i: (i, 0)),
        out_shape=jax.ShapeDtypeStruct((t, d), F32),
        scratch_shapes=[pltpu.VMEM((2, 2 * tm, d), F32), pltpu.SemaphoreType.DMA((2,))],
        compiler_params=_params("arbitrary"),
        name="moe_combine",
    )(pos1, pos2, pos1, pos2, x, route, y)


def _pack_in_proj(w):
    o = 0
    z = w[:, o:o + SSD_INNER]; o += SSD_INNER
    xbc = w[:, o:o + SSD_CONV_CH]; o += SSD_CONV_CH
    dt = w[:, o:o + SSD_HEADS]; o += SSD_HEADS
    uv = w[:, o:o + 2 * SGU_WIDTH]; o += 2 * SGU_WIDTH
    q_lat = w[:, o:o + MLA_Q_RANK]; o += MLA_Q_RANK
    kv_lat = w[:, o:o + MLA_KV_RANK]; o += MLA_KV_RANK
    k_pe = w[:, o:o + MLA_ROPE]; o += MLA_ROPE
    gate = w[:, o:]
    d = w.shape[0]
    zeros = lambda n: jnp.zeros((d, n), w.dtype)
    small = jnp.concatenate([
        q_lat, kv_lat,
        zeros(MLA_NOPE), k_pe, zeros(LANE - MLA_QK),
        zeros(MLA_NOPE), _swap_rope_halves(k_pe), zeros(LANE - MLA_QK),
        dt, zeros(LANE - SSD_HEADS)], axis=1)
    return (jnp.concatenate([z, xbc], axis=1).astype(BF16), uv.astype(BF16), gate.astype(BF16),
            small.astype(BF16))


def _pick(n, prefs):
    for p in prefs:
        if n % p == 0:
            return p
    return n


def kernel(x, mix_norm_g, w_in, conv_w, conv_b, dt_bias, a_log, d_skip, ssd_norm_g, sgu_ln_g, sgu_ln_b, sgu_w,
           sgu_b, q_norm_g, w_uq, kv_norm_g, w_ukv, q_head_g, k_head_g, w_br_ssd, w_br_sgu, w_br_mla, w_out,
           ffn_norm_g, ffn_w_gate, ffn_w_up, ffn_w_down, router_w, moe_w_gate, moe_w_up, moe_w_down):
    batch, seq, d = x.shape
    t = batch * seq
    depth = w_in.shape[0]
    tm = _pick(t, (1024, 512, 256, 128))
    xf = x.reshape(t, d)
    for i in range(depth):
        w_zx, w_uv, w_gate, w_small = _pack_in_proj(w_in[i])
        hn = rmsnorm(xf, mix_norm_g[i], tm)
        zx = matmul(hn, w_zx, BF16, tm, _pick(w_zx.shape[1], (1280,)), "in_proj_ssd")
        uv = matmul(hn, w_uv, BF16, tm, _pick(w_uv.shape[1], (1024,)), "in_proj_sgu")
        gate = matmul(hn, w_gate, BF16, tm, _pick(w_gate.shape[1], (1536,)), "in_proj_gate")
        small = matmul(hn, w_small, F32, tm, SMALL_WIDTH, "in_proj_small")
        y_ssd = ssd_branch(zx, small, conv_w[i], conv_b[i], dt_bias[i], a_log[i], d_skip[i], ssd_norm_g[i],
                           batch, seq, _pick(seq // SSD_CHUNK, (4, 2, 1)))
        y_sgu = sgu_branch(uv, sgu_ln_g[i], sgu_ln_b[i], sgu_w[i], sgu_b[i], _pick(seq // SGU_CHUNK, (4, 2, 1)))
        q, k, v = mla_prep(small, q_norm_g[i], w_uq[i], kv_norm_g[i], w_ukv[i], q_head_g[i], k_head_g[i],
                           batch, seq, _pick(seq, (512, 256, 128)))
        y_mla = flash_attention(q, k, v, batch, seq, _pick(seq, FLASH_TQ), _pick(seq, FLASH_TK)).reshape(t, MLA_WIDTH)
        is_moe = i % 2 == 1
        x1, h2 = merge(xf, y_ssd, y_sgu, y_mla, gate, w_br_ssd[i], w_br_sgu[i], w_br_mla[i], w_out[i],
                       ffn_norm_g[i], _pick(t, (512, 256, 128)), F32 if is_moe else BF16)
        j = i // 2
        if not is_moe:
            xf = ffn(h2, x1, ffn_w_gate[j], ffn_w_up[j], ffn_w_down[j], _pick(t, (512, 256, 128)),
                     _pick(ffn_w_gate.shape[2], (256, 128)))
        else:
            route = router(x1, ffn_norm_g[i], router_w[j], tm)
            tm_e = _pick(t, (512, 256, 128))
            tile_expert, tile_block, n_tiles, pos1, pos2 = _moe_plan(route, t, tm_e)
            xs = moe_dispatch(h2, pos1, pos2, (2 * t // tm_e + N_EXPERTS) * tm_e, tm_e)
            ys = moe_grouped(xs, tile_expert, tile_block, n_tiles, moe_w_gate[j], moe_w_up[j], moe_w_down[j],
                             tm_e, _pick(moe_w_gate.shape[3], (512, 256)))
            xf = moe_combine(x1, route, ys, pos1, pos2, tm_e)
    return xf.reshape(batch, seq, d)
```
